```python
import math
import jax, jax.numpy as jnp
from jax import lax
import numpy as np

D_MODEL = 1024
BATCH = 4
SEQ = 4096
DEPTH = 2

N_A_LAYERS = max(1, DEPTH // 2)
N_B_LAYERS = DEPTH - N_A_LAYERS

A_HEADS = D_MODEL // 128
A_HEAD_DIM = 64
A_V_DIM = 2 * A_HEAD_DIM
A_QK_WIDTH = A_HEADS * 2 * A_HEAD_DIM
A_V_WIDTH = A_HEADS * A_V_DIM
LAMBDA_INIT_BASE = 0.8
LAMBDA_INIT_AMP = 0.6
LAMBDA_INIT_RATE = 0.3

ROPE_THETA = 500000.0
ROT_DIM = A_HEAD_DIM // 4

B_HEADS = D_MODEL // 64
B_HEAD_DIM = 64
B_WIDTH = B_HEADS * B_HEAD_DIM

D_FF = -(-8 * D_MODEL // (3 * 256)) * 256

Q_BLOCK = 128
EPS = 1e-6

kernel_name = 'yoco_diff_stickbreaking_trunk'


def rms_norm(x, g):
    x32 = x.astype(jnp.float32)
    y = x32 * lax.rsqrt(jnp.mean(x32 * x32, axis=-1, keepdims=True) + EPS)
    return (y * g.astype(jnp.float32)).astype(x.dtype)


def rope_tables(seq):
    pos = jnp.arange(seq, dtype=jnp.float32)
    inv_freq = ROPE_THETA ** (-jnp.arange(0, ROT_DIM, 2, dtype=jnp.float32) / ROT_DIM)
    ang = pos[:, None] * inv_freq[None, :]
    return jnp.cos(ang), jnp.sin(ang)


def apply_partial_rope(t, cos, sin):
    half = ROT_DIM // 2
    t1 = t[..., :half].astype(jnp.float32)
    t2 = t[..., half:ROT_DIM].astype(jnp.float32)
    r1 = t1 * cos - t2 * sin
    r2 = t2 * cos + t1 * sin
    return jnp.concatenate([r1.astype(t.dtype), r2.astype(t.dtype), t[..., ROT_DIM:]], axis=-1)


def to_blocks(t):
    b, h, s, d = t.shape
    return t.reshape(b, h, s // Q_BLOCK, Q_BLOCK, d).transpose(2, 0, 1, 3, 4)


def from_blocks(o):
    nb, b, h, qb, d = o.shape
    return o.transpose(1, 0, 3, 2, 4).reshape(b, nb * qb, h, d)


def diff_attention(h, w_qkv, w_o, lq1, lk1, lq2, lk2, subln_g, lambda_init, cos, sin):
    b, s, _ = h.shape
    qkv = h @ w_qkv
    q, k, v = jnp.split(qkv, [A_QK_WIDTH, 2 * A_QK_WIDTH], axis=-1)
    q = q.reshape(b, s, 2 * A_HEADS, A_HEAD_DIM).transpose(0, 2, 1, 3)
    k = k.reshape(b, s, 2 * A_HEADS, A_HEAD_DIM).transpose(0, 2, 1, 3)
    v = v.reshape(b, s, A_HEADS, A_V_DIM).transpose(0, 2, 1, 3)
    q = apply_partial_rope(q, cos, sin)
    k = apply_partial_rope(k, cos, sin)
    f32 = jnp.float32
    lam = (jnp.exp(jnp.sum(lq1.astype(f32) * lk1.astype(f32)))
           - jnp.exp(jnp.sum(lq2.astype(f32) * lk2.astype(f32))) + lambda_init)
    scale = A_HEAD_DIM ** -0.5
    kpos = jnp.arange(s)
    n_blocks = s // Q_BLOCK

    def block(args):
        qb, start = args
        sc = jnp.einsum('bhqd,bhkd->bhqk', qb, k).astype(f32) * scale
        qpos = start + jnp.arange(Q_BLOCK)
        mask = kpos[None, :] <= qpos[:, None]
        p = jax.nn.softmax(jnp.where(mask, sc, -jnp.inf), axis=-1)
        p = p.reshape(b, A_HEADS, 2, Q_BLOCK, s)
        a = p[:, :, 0] - lam * p[:, :, 1]
        return jnp.einsum('bhqk,bhkd->bhqd', a.astype(v.dtype), v)

    o = lax.map(block, (to_blocks(q), jnp.arange(n_blocks) * Q_BLOCK))
    o = from_blocks(o)
    o = rms_norm(o, subln_g) * (1.0 - lambda_init)
    return o.reshape(b, s, A_V_WIDTH).astype(h.dtype) @ w_o


def stick_breaking_attention(h, w_q, w_o, k, v):
    b, s, _ = h.shape
    q = (h @ w_q).reshape(b, s, B_HEADS, B_HEAD_DIM).transpose(0, 2, 1, 3)
    f32 = jnp.float32
    scale = B_HEAD_DIM ** -0.5
    kpos = jnp.arange(s)
    n_blocks = s // Q_BLOCK

    def block(args):
        qb, start = args
        z = jnp.einsum('bhqd,bhkd->bhqk', qb, k).astype(f32) * scale
        qpos = start + jnp.arange(Q_BLOCK)
        mask = kpos[None, :] < qpos[:, None]
        log_beta = jax.nn.log_sigmoid(z)
        log_1m = jnp.where(mask, jax.nn.log_sigmoid(-z), 0.0)
        between = lax.cumsum(log_1m, axis=3, reverse=True) - log_1m
        a = jnp.where(mask, jnp.exp(log_beta + between), 0.0)
        return jnp.einsum('bhqk,bhkd->bhqd', a.astype(v.dtype), v)

    o = lax.map(block, (to_blocks(q), jnp.arange(n_blocks) * Q_BLOCK))
    o = from_blocks(o).reshape(b, s, B_WIDTH)
    return o @ w_o


def swiglu(h, w_gate_up, w_down):
    gate, up = jnp.split(h @ w_gate_up, 2, axis=-1)
    return (jax.nn.silu(gate) * up) @ w_down


def setup_inputs(seed: int = 0) -> dict:
    key = jax.random.key(seed)
    ks = jax.random.split(key, 24)
    f32 = jnp.float32

    def w(k, shape, fan_in):
        return jax.random.normal(k, shape, f32) * (fan_in ** -0.5)

    def gain(k, shape):
        return 1.0 + 0.02 * jax.random.normal(k, shape, f32)

    return {
        'x': jax.random.normal(ks[0], (BATCH, SEQ, D_MODEL), f32),
        'a_w_qkv': w(ks[1], (N_A_LAYERS, D_MODEL, 2 * A_QK_WIDTH + A_V_WIDTH), D_MODEL),
        'a_w_o': w(ks[2], (N_A_LAYERS, A_V_WIDTH, D_MODEL), A_V_WIDTH),
        'a_lambda_q1': 0.1 * jax.random.normal(ks[3], (N_A_LAYERS, A_HEAD_DIM), f32),
        'a_lambda_k1': 0.1 * jax.random.normal(ks[4], (N_A_LAYERS, A_HEAD_DIM), f32),
        'a_lambda_q2': 0.1 * jax.random.normal(ks[5], (N_A_LAYERS, A_HEAD_DIM), f32),
        'a_lambda_k2': 0.1 * jax.random.normal(ks[6], (N_A_LAYERS, A_HEAD_DIM), f32),
        'a_subln_g': gain(ks[7], (N_A_LAYERS, A_V_DIM)),
        'kv_norm_g': gain(ks[8], (D_MODEL,)),
        'kv_w': w(ks[9], (D_MODEL, 2 * B_WIDTH), D_MODEL),
        'b_w_q': w(ks[10], (N_B_LAYERS, D_MODEL, B_WIDTH), D_MODEL),
        'b_w_o': w(ks[11], (N_B_LAYERS, B_WIDTH, D_MODEL), B_WIDTH),
        'mix_pre_g': gain(ks[12], (DEPTH, D_MODEL)),
        'mix_post_g': gain(ks[13], (DEPTH, D_MODEL)),
        'ffn_pre_g': gain(ks[14], (DEPTH, D_MODEL)),
        'ffn_post_g': gain(ks[15], (DEPTH, D_MODEL)),
        'ffn_w_gate_up': w(ks[16], (DEPTH, D_MODEL, 2 * D_FF), D_MODEL),
        'ffn_w_down': w(ks[17], (DEPTH, D_FF, D_MODEL), D_FF),
    }


def reference(x, a_w_qkv, a_w_o, a_lambda_q1, a_lambda_k1, a_lambda_q2, a_lambda_k2,
              a_subln_g, kv_norm_g, kv_w, b_w_q, b_w_o, mix_pre_g, mix_post_g,
              ffn_pre_g, ffn_post_g, ffn_w_gate_up, ffn_w_down):
    b, s, _ = x.shape
    cos, sin = rope_tables(s)
    shared_k = None
    shared_v = None
    for layer in range(DEPTH):
        h = rms_norm(x, mix_pre_g[layer])
        if layer < N_A_LAYERS:
            lambda_init = LAMBDA_INIT_BASE - LAMBDA_INIT_AMP * math.exp(-LAMBDA_INIT_RATE * layer)
            m = diff_attention(h, a_w_qkv[layer], a_w_o[layer],
                               a_lambda_q1[layer], a_lambda_k1[layer],
                               a_lambda_q2[layer], a_lambda_k2[layer],
                               a_subln_g[layer], lambda_init, cos, sin)
        else:
            if shared_k is None:
                kv = rms_norm(x, kv_norm_g) @ kv_w
                kk, vv = jnp.split(kv, 2, axis=-1)
                shared_k = kk.reshape(b, s, B_HEADS, B_HEAD_DIM).transpose(0, 2, 1, 3)
                shared_v = vv.reshape(b, s, B_HEADS, B_HEAD_DIM).transpose(0, 2, 1, 3)
            j = layer - N_A_LAYERS
            m = stick_breaking_attention(h, b_w_q[j], b_w_o[j], shared_k, shared_v)
        x = x + rms_norm(m, mix_post_g[layer])
        f = swiglu(rms_norm(x, ffn_pre_g[layer]), ffn_w_gate_up[layer], ffn_w_down[layer])
        x = x + rms_norm(f, ffn_post_g[layer])
    return x
```

```python
import functools
import math

import jax
import jax.numpy as jnp
from jax import lax
from jax.experimental import pallas as pl
from jax.experimental.pallas import tpu as pltpu

F32 = jnp.float32
BF16 = jnp.bfloat16

EPS = 1e-6
ROPE_THETA = 500000.0
ROT_HALF = 8
HEAD_DIM = 64
PAIR = 2 * HEAD_DIM
LAMBDA_INIT_BASE, LAMBDA_INIT_AMP, LAMBDA_INIT_RATE = 0.8, 0.6, 0.3

ATT_BLOCK = 256
PROJ_ROWS = 512
POST_ROWS = 512
VMEM_LIMIT = 56 * 1024 * 1024

_NT = (((1,), (1,)), ((), ()))


def _rms(x):
    return x * lax.rsqrt(jnp.mean(x * x, axis=-1, keepdims=True) + EPS)


def _proj_kernel(x_ref, gq_ref, gkv_ref, wqT_ref, wk_ref, wvT_ref, *rest, rope):
    if rope:
        cosT_ref, sinT_ref, ck_ref, sk1_ref, sk2_ref, qT_ref, k_ref, vT_ref = rest
    else:
        qT_ref, k_ref, vT_ref = rest
    tm = x_ref.shape[1]
    d = x_ref.shape[2]
    n_pairs = d // PAIR
    xn = _rms(x_ref[0])
    hq = (xn * gq_ref[...]).astype(BF16)
    hkv = (xn * gkv_ref[...]).astype(BF16)

    qT = lax.dot_general(wqT_ref[...], hq, _NT, preferred_element_type=F32)
    scale = HEAD_DIM ** -0.5
    for j in range(n_pairs):
        blk = qT[j * PAIR:(j + 1) * PAIR]
        if rope:
            cos, sin = cosT_ref[...], sinT_ref[...]
            parts = []
            for c in range(2):
                b = c * HEAD_DIM
                t1, t2 = blk[b:b + ROT_HALF], blk[b + ROT_HALF:b + 2 * ROT_HALF]
                parts += [t1 * cos - t2 * sin, t2 * cos + t1 * sin,
                          blk[b + 2 * ROT_HALF:b + HEAD_DIM]]
            blk = jnp.concatenate(parts, axis=0)
        blk = (blk * scale).astype(BF16)
        for t in range(tm // ATT_BLOCK):
            qT_ref[0, j, t] = blk[:, t * ATT_BLOCK:(t + 1) * ATT_BLOCK]

    kf = jnp.dot(hkv, wk_ref[...], preferred_element_type=F32)
    for j in range(n_pairs):
        kb = kf[:, j * PAIR:(j + 1) * PAIR]
        if rope:
            kb = (kb * ck_ref[...] + pltpu.roll(kb, PAIR - ROT_HALF, 1) * sk1_ref[...]
                  + pltpu.roll(kb, ROT_HALF, 1) * sk2_ref[...])
        k_ref[0, :, j * PAIR:(j + 1) * PAIR] = kb.astype(BF16)

    vT = lax.dot_general(wvT_ref[...], hkv, _NT, preferred_element_type=F32).astype(BF16)
    for j in range(n_pairs):
        for t in range(tm // ATT_BLOCK):
            vT_ref[0, j, t] = vT[j * PAIR:(j + 1) * PAIR, t * ATT_BLOCK:(t + 1) * ATT_BLOCK]


def _project(x, gq, gkv, wqT, wk, wvT, rope_tables=None):
    b, s, d = x.shape
    tm = PROJ_ROWS
    n_pairs = d // PAIR
    nblk = s // ATT_BLOCK
    per_step = tm // ATT_BLOCK
    const = lambda bi, i: (0, 0)
    in_specs = [
        pl.BlockSpec((1, tm, d), lambda bi, i: (bi, i, 0)),
        pl.BlockSpec((1, d), const),
        pl.BlockSpec((1, d), const),
        pl.BlockSpec((d, d), const),
        pl.BlockSpec((d, d), const),
        pl.BlockSpec((d, d), const),
    ]
    args = [x, gq.reshape(1, d), gkv.reshape(1, d), wqT, wk, wvT]
    if rope_tables is not None:
        cosT, sinT, ck, sk1, sk2 = rope_tables
        in_specs += [
            pl.BlockSpec((ROT_HALF, tm), lambda bi, i: (0, i)),
            pl.BlockSpec((ROT_HALF, tm), lambda bi, i: (0, i)),
            pl.BlockSpec((tm, PAIR), lambda bi, i: (i, 0)),
            pl.BlockSpec((tm, PAIR), lambda bi, i: (i, 0)),
            pl.BlockSpec((tm, PAIR), lambda bi, i: (i, 0)),
        ]
        args += [cosT, sinT, ck, sk1, sk2]
    t_spec = pl.BlockSpec((1, n_pairs, per_step, PAIR, ATT_BLOCK), lambda bi, i: (bi, 0, i, 0, 0))
    t_shape = jax.ShapeDtypeStruct((b, n_pairs, nblk, PAIR, ATT_BLOCK), BF16)
    return pl.pallas_call(
        functools.partial(_proj_kernel, rope=rope_tables is not None),
        grid=(b, s // tm),
        in_specs=in_specs,
        out_specs=[t_spec, pl.BlockSpec((1, tm, d), lambda bi, i: (bi, i, 0)), t_spec],
        out_shape=[t_shape, jax.ShapeDtypeStruct((b, s, d), BF16), t_shape],
        compiler_params=pltpu.CompilerParams(
            dimension_semantics=("arbitrary", "arbitrary"), vmem_limit_bytes=VMEM_LIMIT),
        name="proj_rope" if rope_tables is not None else "proj",
    )(*args)


def _load_qpad(qT_ref, qi, qpad_sc):
    qT = qT_ref[0, 0, qi].astype(F32)
    row = lax.broadcasted_iota(jnp.int32, qT.shape, 0)
    first = row < HEAD_DIM
    qpad_sc[:, :ATT_BLOCK] = jnp.where(first, qT, 0.0).astype(BF16)
    qpad_sc[:, ATT_BLOCK:] = jnp.where(first, 0.0, qT).astype(BF16)


def _block_positions():
    krow = lax.broadcasted_iota(jnp.int32, (ATT_BLOCK, 2 * ATT_BLOCK), 0)
    qcol = lax.broadcasted_iota(jnp.int32, (ATT_BLOCK, 2 * ATT_BLOCK), 1)
    qcol = jnp.where(qcol >= ATT_BLOCK, qcol - ATT_BLOCK, qcol)
    return krow, qcol


def _diff_attn_kernel(lam_ref, g_ref, qT_ref, k_ref, vT_ref, o_ref,
                      qpad_sc, m_sc, l_sc, acc_sc, *, lambda_init):
    blk = ATT_BLOCK
    n_q = qT_ref.shape[2]
    lam4 = lam_ref[...]
    lam = (jnp.exp(jnp.sum(lam4[0:1] * lam4[1:2], axis=1, keepdims=True))
           - jnp.exp(jnp.sum(lam4[2:3] * lam4[3:4], axis=1, keepdims=True)) + lambda_init)

    def kv_block(ki, causal_edge):
        kb = k_ref[0, pl.ds(pl.multiple_of(ki * blk, blk), blk), :]
        s = jnp.dot(kb, qpad_sc[...], preferred_element_type=F32)
        if causal_edge:
            krow, qcol = _block_positions()
            s = jnp.where(krow <= qcol, s, -jnp.inf)
        m_prev = m_sc[...]
        m_new = jnp.maximum(m_prev, jnp.max(s, axis=0, keepdims=True))
        alpha = jnp.exp(m_prev - m_new)
        p = jnp.exp(s - m_new)
        l_sc[...] = alpha * l_sc[...] + jnp.sum(p, axis=0, keepdims=True)
        acc_sc[...] = alpha * acc_sc[...] + jnp.dot(
            vT_ref[0, 0, ki], p.astype(BF16), preferred_element_type=F32)
        m_sc[...] = m_new

    def q_block(qi, carry):
        _load_qpad(qT_ref, qi, qpad_sc)
        m_sc[...] = jnp.full(m_sc.shape, -jnp.inf, F32)
        l_sc[...] = jnp.zeros(l_sc.shape, F32)
        acc_sc[...] = jnp.zeros(acc_sc.shape, F32)

        def full_block(ki, c):
            kv_block(ki, False)
            return c
        lax.fori_loop(0, qi, full_block, 0)
        kv_block(qi, True)

        acc, l = acc_sc[...], l_sc[...]
        o = acc[:, :blk] / l[:, :blk] - lam * (acc[:, blk:] / l[:, blk:])
        o = o * lax.rsqrt(jnp.mean(o * o, axis=0, keepdims=True) + EPS)
        o = o * g_ref[...] * (1.0 - lambda_init)
        o_ref[0, pl.ds(pl.multiple_of(qi * blk, blk), blk), :] = o.T.astype(BF16)
        return carry

    lax.fori_loop(0, n_q, q_block, 0)


def _stick_attn_kernel(qT_ref, k_ref, vT_ref, o_ref, qpad_sc, u_sc, c_sc, acc_sc):
    blk = ATT_BLOCK
    n_q = qT_ref.shape[2]
    r = lax.broadcasted_iota(jnp.int32, (blk, blk), 0)
    c = lax.broadcasted_iota(jnp.int32, (blk, blk), 1)
    u_sc[...] = jnp.where(c > r, 1.0, 0.0).astype(BF16)

    def kv_block(ki, causal_edge):
        kb = k_ref[0, pl.ds(pl.multiple_of(ki * blk, blk), blk), :]
        z = jnp.dot(kb, qpad_sc[...], preferred_element_type=F32)
        softplus_neg_abs = jnp.log(1.0 + jnp.exp(-jnp.abs(z)))
        log_beta = jnp.minimum(z, 0.0) - softplus_neg_abs
        log_1m = log_beta - z
        if causal_edge:
            krow, qcol = _block_positions()
            keep = krow < qcol
            log_1m = jnp.where(keep, log_1m, 0.0)
        hi = log_1m.astype(BF16)
        lo = (log_1m - hi.astype(F32)).astype(BF16)
        u = u_sc[...]
        between = (jnp.dot(u, hi, preferred_element_type=F32)
                   + jnp.dot(u, lo, preferred_element_type=F32))
        a = jnp.exp(log_beta + between + c_sc[...])
        if causal_edge:
            a = jnp.where(keep, a, 0.0)
        a = a.astype(BF16)
        vT = vT_ref[0, 0, ki]
        acc_sc[:HEAD_DIM] += jnp.dot(vT[:HEAD_DIM], a[:, :blk], preferred_element_type=F32)
        acc_sc[HEAD_DIM:] += jnp.dot(vT[HEAD_DIM:], a[:, blk:], preferred_element_type=F32)
        c_sc[...] += between[0:1] + log_1m[0:1]

    def q_block(qi, carry):
        _load_qpad(qT_ref, qi, qpad_sc)
        c_sc[...] = jnp.zeros(c_sc.shape, F32)
        acc_sc[...] = jnp.zeros(acc_sc.shape, F32)
        kv_block(qi, True)

        def full_block(i, cc):
            kv_block(qi - 1 - i, False)
            return cc
        lax.fori_loop(0, qi, full_block, 0)
        o_ref[0, pl.ds(pl.multiple_of(qi * blk, blk), blk), :] = acc_sc[...].T.astype(BF16)
        return carry

    lax.fori_loop(0, n_q, q_block, 0)


def _attention(qT, k, vT, *, lam4=None, subln_g=None, lambda_init=None):
    b, n_pairs, nblk, _, blk = qT.shape
    s = nblk * blk
    d = k.shape[-1]
    qv_spec = pl.BlockSpec((1, 1, nblk, PAIR, blk), lambda bi, h: (bi, h, 0, 0, 0))
    k_spec = pl.BlockSpec((1, s, PAIR), lambda bi, h: (bi, 0, h))
    o_spec = pl.BlockSpec((1, s, PAIR), lambda bi, h: (bi, 0, h))
    common = dict(
        grid=(b, n_pairs),
        out_specs=o_spec,
        out_shape=jax.ShapeDtypeStruct((b, s, d), BF16),
        compiler_params=pltpu.CompilerParams(
            dimension_semantics=("arbitrary", "arbitrary"), vmem_limit_bytes=VMEM_LIMIT),
    )
    qpad = pltpu.VMEM((PAIR, 2 * blk), BF16)
    if lam4 is not None:
        return pl.pallas_call(
            functools.partial(_diff_attn_kernel, lambda_init=lambda_init),
            in_specs=[pl.BlockSpec(lam4.shape, lambda bi, h: (0, 0)),
                      pl.BlockSpec((PAIR, 1), lambda bi, h: (0, 0)),
                      qv_spec, k_spec, qv_spec],
            scratch_shapes=[qpad, pltpu.VMEM((1, 2 * blk), F32), pltpu.VMEM((1, 2 * blk), F32),
                            pltpu.VMEM((PAIR, 2 * blk), F32)],
            name="diff_attention", **common,
        )(lam4, subln_g.reshape(PAIR, 1), qT, k, vT)
    return pl.pallas_call(
        _stick_attn_kernel,
        in_specs=[qv_spec, k_spec, qv_spec],
        scratch_shapes=[qpad, pltpu.VMEM((blk, blk), BF16), pltpu.VMEM((1, 2 * blk), F32),
                        pltpu.VMEM((PAIR, blk), F32)],
        name="stick_attention", **common,
    )(qT, k, vT)


def _post_kernel(o_ref, x_ref, wo_ref, gpost_ref, gpre_ref, wgu_ref, wd_ref, gfpost_ref, out_ref,
                 *, ff_chunk):
    m = jnp.dot(o_ref[...], wo_ref[...], preferred_element_type=F32)
    x1 = x_ref[...] + _rms(m) * gpost_ref[...]
    hf = (_rms(x1) * gpre_ref[...]).astype(BF16)
    d_ff = wd_ref.shape[0]
    f = jnp.zeros(x1.shape, F32)
    for j in range(d_ff // ff_chunk):
        lo = j * ff_chunk
        gate = jnp.dot(hf, wgu_ref[:, lo:lo + ff_chunk], preferred_element_type=F32)
        up = jnp.dot(hf, wgu_ref[:, d_ff + lo:d_ff + lo + ff_chunk], preferred_element_type=F32)
        h = (gate * jax.nn.sigmoid(gate) * up).astype(BF16)
        f = f + jnp.dot(h, wd_ref[lo:lo + ff_chunk, :], preferred_element_type=F32)
    out_ref[...] = x1 + _rms(f) * gfpost_ref[...]


def _post(o, x, wo, gpost, gpre, wgu, wd, gfpost):
    n, d = x.shape
    d_ff = wd.shape[0]
    tm = POST_ROWS
    ff_chunk = d_ff // 2 if (d_ff // 2) % 128 == 0 else d_ff
    rows = lambda i: (i, 0)
    const = lambda i: (0, 0)
    resident = functools.partial(pl.BlockSpec, index_map=const, pipeline_mode=pl.Buffered(1))
    gain = pl.BlockSpec((1, d), const)
    return pl.pallas_call(
        functools.partial(_post_kernel, ff_chunk=ff_chunk),
        grid=(n // tm,),
        in_specs=[pl.BlockSpec((tm, d), rows), pl.BlockSpec((tm, d), rows),
                  resident((d, d)), gain, gain,
                  resident((d, 2 * d_ff)), resident((d_ff, d)), gain],
        out_specs=pl.BlockSpec((tm, d), rows),
        out_shape=jax.ShapeDtypeStruct((n, d), F32),
        compiler_params=pltpu.CompilerParams(
            dimension_semantics=("arbitrary",), vmem_limit_bytes=VMEM_LIMIT),
        name="post_swiglu",
    )(o, x, wo, gpost.reshape(1, d), gpre.reshape(1, d), wgu, wd, gfpost.reshape(1, d))


def _rope_tables(seq):
    pos = jnp.arange(seq, dtype=F32)
    inv_freq = ROPE_THETA ** (-jnp.arange(0, 2 * ROT_HALF, 2, dtype=F32) / (2 * ROT_HALF))
    ang = pos[:, None] * inv_freq[None, :]
    cos, sin = jnp.cos(ang), jnp.sin(ang)
    ones = jnp.ones((seq, HEAD_DIM - 2 * ROT_HALF), F32)
    zeros = jnp.zeros((seq, HEAD_DIM - 2 * ROT_HALF), F32)
    z8 = jnp.zeros_like(sin)
    ck = jnp.tile(jnp.concatenate([cos, cos, ones], axis=1), (1, 2))
    sk1 = jnp.tile(jnp.concatenate([-sin, z8, zeros], axis=1), (1, 2))
    sk2 = jnp.tile(jnp.concatenate([z8, sin, zeros], axis=1), (1, 2))
    return cos.T, sin.T, ck, sk1, sk2


def kernel(x, a_w_qkv, a_w_o, a_lambda_q1, a_lambda_k1, a_lambda_q2, a_lambda_k2, a_subln_g,
           kv_norm_g, kv_w, b_w_q, b_w_o, mix_pre_g, mix_post_g, ffn_pre_g, ffn_post_g,
           ffn_w_gate_up, ffn_w_down):
    b, s, d = x.shape
    assert a_w_qkv.shape[0] == 1 and b_w_q.shape[0] == 1 and mix_pre_g.shape[0] == 2
    assert s % PROJ_ROWS == 0 and (b * s) % POST_ROWS == 0 and d % PAIR == 0

    w = a_w_qkv[0]
    q, k, vT = _project(x, mix_pre_g[0], mix_pre_g[0],
                        w[:, :d].T.astype(BF16), w[:, d:2 * d].astype(BF16),
                        w[:, 2 * d:].T.astype(BF16), rope_tables=_rope_tables(s))
    lam4 = jnp.stack([a_lambda_q1[0], a_lambda_k1[0], a_lambda_q2[0], a_lambda_k2[0]])
    lambda_init = LAMBDA_INIT_BASE - LAMBDA_INIT_AMP * math.exp(-LAMBDA_INIT_RATE * 0)
    o = _attention(q, k, vT, lam4=lam4, subln_g=a_subln_g[0], lambda_init=lambda_init)
    x = _post(o.reshape(b * s, d), x.reshape(b * s, d), a_w_o[0].astype(BF16), mix_post_g[0],
              ffn_pre_g[0], ffn_w_gate_up[0].astype(BF16), ffn_w_down[0].astype(BF16),
              ffn_post_g[0]).reshape(b, s, d)

    q, k, vT = _project(x, mix_pre_g[1], kv_norm_g,
                        b_w_q[0].T.astype(BF16), kv_w[:, :d].astype(BF16),
                        kv_w[:, d:].T.astype(BF16))
    o = _attention(q, k, vT)
    x = _post(o.reshape(b * s, d), x.reshape(b * s, d), b_w_o[0].astype(BF16), mix_post_g[1],
              ffn_pre_g[1], ffn_w_gate_up[1].astype(BF16), ffn_w_down[1].astype(BF16),
              ffn_post_g[1]).reshape(b, s, d)
    return x
```

```python
import functools
import math

import jax
import jax.numpy as jnp
from jax import lax
from jax.experimental import pallas as pl
from jax.experimental.pallas import tpu as pltpu

F32 = jnp.float32
BF16 = jnp.bfloat16

EPS = 1e-6
ROPE_THETA = 500000.0
ROT_HALF = 8
HEAD_DIM = 64
PAIR = 2 * HEAD_DIM
LAMBDA_INIT_BASE, LAMBDA_INIT_AMP, LAMBDA_INIT_RATE = 0.8, 0.6, 0.3
LOG2_E = math.log2(math.e)

ATT_BLOCK = 256
ATT_GROUP = 4
PROJ_ROWS = 512
POST_ROWS = 512
VMEM_LIMIT = 56 * 1024 * 1024

_NT = (((1,), (1,)), ((), ()))


def _rms(x):
    return x * lax.rsqrt(jnp.mean(x * x, axis=-1, keepdims=True) + EPS)


def _proj_kernel(x_ref, gq_ref, gkv_ref, wqT_ref, wk_ref, wvT_ref, *rest, rope):
    if rope:
        cosT_ref, sinT_ref, ck_ref, sk1_ref, sk2_ref, qT_ref, k_ref, vT_ref = rest
    else:
        qT_ref, k_ref, vT_ref = rest
    tm = x_ref.shape[1]
    d = x_ref.shape[2]
    n_pairs = d // PAIR
    xn = _rms(x_ref[0])
    hq = (xn * gq_ref[...]).astype(BF16)
    hkv = (xn * gkv_ref[...]).astype(BF16)

    qT = lax.dot_general(wqT_ref[...], hq, _NT, preferred_element_type=F32)
    scale = HEAD_DIM ** -0.5 * LOG2_E
    for j in range(n_pairs):
        blk = qT[j * PAIR:(j + 1) * PAIR]
        if rope:
            cos, sin = cosT_ref[...], sinT_ref[...]
            parts = []
            for c in range(2):
                b = c * HEAD_DIM
                t1, t2 = blk[b:b + ROT_HALF], blk[b + ROT_HALF:b + 2 * ROT_HALF]
                parts += [t1 * cos - t2 * sin, t2 * cos + t1 * sin,
                          blk[b + 2 * ROT_HALF:b + HEAD_DIM]]
            blk = jnp.concatenate(parts, axis=0)
        blk = (blk * scale).astype(BF16)
        for t in range(tm // ATT_BLOCK):
            qT_ref[0, j, t] = blk[:, t * ATT_BLOCK:(t + 1) * ATT_BLOCK]

    kf = jnp.dot(hkv, wk_ref[...], preferred_element_type=F32)
    for j in range(n_pairs):
        kb = kf[:, j * PAIR:(j + 1) * PAIR]
        if rope:
            kb = (kb * ck_ref[...] + pltpu.roll(kb, PAIR - ROT_HALF, 1) * sk1_ref[...]
                  + pltpu.roll(kb, ROT_HALF, 1) * sk2_ref[...])
        k_ref[0, :, j * PAIR:(j + 1) * PAIR] = kb.astype(BF16)

    vT = lax.dot_general(wvT_ref[...], hkv, _NT, preferred_element_type=F32).astype(BF16)
    for j in range(n_pairs):
        for t in range(tm // ATT_BLOCK):
            vT_ref[0, j, t] = vT[j * PAIR:(j + 1) * PAIR, t * ATT_BLOCK:(t + 1) * ATT_BLOCK]


def _project(x, gq, gkv, wqT, wk, wvT, rope_tables=None):
    b, s, d = x.shape
    tm = PROJ_ROWS
    n_pairs = d // PAIR
    nblk = s // ATT_BLOCK
    per_step = tm // ATT_BLOCK
    const = lambda bi, i: (0, 0)
    in_specs = [
        pl.BlockSpec((1, tm, d), lambda bi, i: (bi, i, 0)),
        pl.BlockSpec((1, d), const),
        pl.BlockSpec((1, d), const),
        pl.BlockSpec((d, d), const),
        pl.BlockSpec((d, d), const),
        pl.BlockSpec((d, d), const),
    ]
    args = [x, gq.reshape(1, d), gkv.reshape(1, d), wqT, wk, wvT]
    if rope_tables is not None:
        cosT, sinT, ck, sk1, sk2 = rope_tables
        in_specs += [
            pl.BlockSpec((ROT_HALF, tm), lambda bi, i: (0, i)),
            pl.BlockSpec((ROT_HALF, tm), lambda bi, i: (0, i)),
            pl.BlockSpec((tm, PAIR), lambda bi, i: (i, 0)),
            pl.BlockSpec((tm, PAIR), lambda bi, i: (i, 0)),
            pl.BlockSpec((tm, PAIR), lambda bi, i: (i, 0)),
        ]
        args += [cosT, sinT, ck, sk1, sk2]
    t_spec = pl.BlockSpec((1, n_pairs, per_step, PAIR, ATT_BLOCK), lambda bi, i: (bi, 0, i, 0, 0))
    t_shape = jax.ShapeDtypeStruct((b, n_pairs, nblk, PAIR, ATT_BLOCK), BF16)
    return pl.pallas_call(
        functools.partial(_proj_kernel, rope=rope_tables is not None),
        grid=(b, s // tm),
        in_specs=in_specs,
        out_specs=[t_spec, pl.BlockSpec((1, tm, d), lambda bi, i: (bi, i, 0)), t_spec],
        out_shape=[t_shape, jax.ShapeDtypeStruct((b, s, d), BF16), t_shape],
        compiler_params=pltpu.CompilerParams(
            dimension_semantics=("arbitrary", "arbitrary"), vmem_limit_bytes=VMEM_LIMIT),
        name="proj_rope" if rope_tables is not None else "proj",
    )(*args)


def _load_qpad(qT_ref, g, qi, qpad_sc):
    qT = qT_ref[0, g, qi].astype(F32)
    row = lax.broadcasted_iota(jnp.int32, qT.shape, 0)
    first = row < HEAD_DIM
    qpad_sc[g, :, :ATT_BLOCK] = jnp.where(first, qT, 0.0).astype(BF16)
    qpad_sc[g, :, ATT_BLOCK:] = jnp.where(first, 0.0, qT).astype(BF16)


def _block_positions():
    krow = lax.broadcasted_iota(jnp.int32, (ATT_BLOCK, 2 * ATT_BLOCK), 0)
    qcol = lax.broadcasted_iota(jnp.int32, (ATT_BLOCK, 2 * ATT_BLOCK), 1)
    qcol = jnp.where(qcol >= ATT_BLOCK, qcol - ATT_BLOCK, qcol)
    return krow, qcol


def _key_block(k_ref, g, ki):
    start = pl.multiple_of(ki * ATT_BLOCK, ATT_BLOCK)
    return k_ref[0, pl.ds(start, ATT_BLOCK), g * PAIR:(g + 1) * PAIR]


def _emit_skewed(stages, group, *args):
    vals = [None] * group
    for step in range(group + len(stages) - 1):
        for k, stage in enumerate(stages):
            g = step - k
            if 0 <= g < group:
                vals[g] = stage(g, vals[g], *args)


def _diff_attn_kernel(lam_ref, g_ref, qT_ref, k_ref, vT_ref, o_ref,
                      qpad_sc, m_sc, l_sc, acc_sc, *, lambda_init):
    blk = ATT_BLOCK
    group, n_q = qT_ref.shape[1], qT_ref.shape[2]
    lam4 = lam_ref[...]
    lam = (jnp.exp(jnp.sum(lam4[0:1] * lam4[1:2], axis=1, keepdims=True))
           - jnp.exp(jnp.sum(lam4[2:3] * lam4[3:4], axis=1, keepdims=True)) + lambda_init)

    def scores(g, _, ki, causal_edge):
        s = jnp.dot(_key_block(k_ref, g, ki), qpad_sc[g], preferred_element_type=F32)
        if causal_edge:
            krow, qcol = _block_positions()
            s = jnp.where(krow <= qcol, s, -jnp.inf)
        return s

    def softmax_step(g, s, ki, causal_edge):
        m_prev = m_sc[g]
        m_new = jnp.maximum(m_prev, jnp.max(s, axis=0, keepdims=True))
        alpha = jnp.exp2(m_prev - m_new)
        p = jnp.exp2(s - m_new)
        l_sc[g] = alpha * l_sc[g] + jnp.sum(p, axis=0, keepdims=True)
        m_sc[g] = m_new
        return alpha, p.astype(BF16)

    def accumulate(g, alpha_p, ki, causal_edge):
        alpha, p = alpha_p
        acc_sc[g] = alpha * acc_sc[g] + jnp.dot(
            vT_ref[0, g, ki], p, preferred_element_type=F32)

    def kv_blocks(ki, causal_edge):
        _emit_skewed([scores, softmax_step, accumulate], group, ki, causal_edge)

    def q_block(qi, carry):
        for g in range(group):
            _load_qpad(qT_ref, g, qi, qpad_sc)
        m_sc[...] = jnp.full(m_sc.shape, -jnp.inf, F32)
        l_sc[...] = jnp.zeros(l_sc.shape, F32)
        acc_sc[...] = jnp.zeros(acc_sc.shape, F32)

        def full_blocks(ki, c):
            kv_blocks(ki, False)
            return c
        lax.fori_loop(0, qi, full_blocks, 0)
        kv_blocks(qi, True)

        rows = pl.ds(pl.multiple_of(qi * blk, blk), blk)
        for g in range(group):
            acc, l = acc_sc[g], l_sc[g]
            o = acc[:, :blk] / l[:, :blk] - lam * (acc[:, blk:] / l[:, blk:])
            o = o * lax.rsqrt(jnp.mean(o * o, axis=0, keepdims=True) + EPS)
            o = o * g_ref[...] * (1.0 - lambda_init)
            o_ref[0, rows, g * PAIR:(g + 1) * PAIR] = o.T.astype(BF16)
        return carry

    lax.fori_loop(0, n_q, q_block, 0)


def _stick_attn_kernel(qT_ref, k_ref, vT_ref, o_ref, qpad_sc, u_sc, c_sc, acc_sc):
    blk = ATT_BLOCK
    group, n_q = qT_ref.shape[1], qT_ref.shape[2]
    r = lax.broadcasted_iota(jnp.int32, (blk, blk), 0)
    c = lax.broadcasted_iota(jnp.int32, (blk, blk), 1)
    u_sc[...] = jnp.where(c > r, 1.0, 0.0).astype(BF16)

    def scores(g, _, ki, causal_edge):
        return jnp.dot(_key_block(k_ref, g, ki), qpad_sc[g], preferred_element_type=F32)

    def log_sigmoids(g, z, ki, causal_edge):
        log_denom = jnp.log(1.0 + jnp.exp2(-jnp.abs(z))) * LOG2_E
        log_beta = jnp.minimum(z, 0.0) - log_denom
        log_1m = log_beta - z
        if causal_edge:
            krow, qcol = _block_positions()
            log_1m = jnp.where(krow < qcol, log_1m, 0.0)
        hi = log_1m.astype(BF16)
        lo = (log_1m - hi.astype(F32)).astype(BF16)
        return log_beta, log_1m[0:1], hi, lo

    def suffix_sums(g, vals, ki, causal_edge):
        log_beta, first_row, hi, lo = vals
        u = u_sc[...]
        between = (jnp.dot(u, hi, preferred_element_type=F32)
                   + jnp.dot(u, lo, preferred_element_type=F32))
        return log_beta, first_row, between

    def weights(g, vals, ki, causal_edge):
        log_beta, first_row, between = vals
        a = jnp.exp2(log_beta + between + c_sc[g])
        if causal_edge:
            krow, qcol = _block_positions()
            a = jnp.where(krow < qcol, a, 0.0)
        c_sc[g] += between[0:1] + first_row
        return a.astype(BF16)

    def accumulate(g, a, ki, causal_edge):
        vT = vT_ref[0, g, ki]
        acc_sc[g, :HEAD_DIM] += jnp.dot(vT[:HEAD_DIM], a[:, :blk], preferred_element_type=F32)
        acc_sc[g, HEAD_DIM:] += jnp.dot(vT[HEAD_DIM:], a[:, blk:], preferred_element_type=F32)

    def kv_blocks(ki, causal_edge):
        _emit_skewed([scores, log_sigmoids, suffix_sums, weights, accumulate], group,
                     ki, causal_edge)

    def q_block(qi, carry):
        for g in range(group):
            _load_qpad(qT_ref, g, qi, qpad_sc)
        c_sc[...] = jnp.zeros(c_sc.shape, F32)
        acc_sc[...] = jnp.zeros(acc_sc.shape, F32)
        kv_blocks(qi, True)

        def full_blocks(i, cc):
            kv_blocks(qi - 1 - i, False)
            return cc
        lax.fori_loop(0, qi, full_blocks, 0)
        rows = pl.ds(pl.multiple_of(qi * blk, blk), blk)
        for g in range(group):
            o_ref[0, rows, g * PAIR:(g + 1) * PAIR] = acc_sc[g].T.astype(BF16)
        return carry

    lax.fori_loop(0, n_q, q_block, 0)


def _attention(qT, k, vT, *, lam4=None, subln_g=None, lambda_init=None):
    b, n_pairs, nblk, _, blk = qT.shape
    s = nblk * blk
    d = k.shape[-1]
    group = ATT_GROUP
    qv_spec = pl.BlockSpec((1, group, nblk, PAIR, blk), lambda bi, h: (bi, h, 0, 0, 0))
    k_spec = pl.BlockSpec((1, s, group * PAIR), lambda bi, h: (bi, 0, h))
    o_spec = pl.BlockSpec((1, s, group * PAIR), lambda bi, h: (bi, 0, h))
    common = dict(
        grid=(b, n_pairs // group),
        out_specs=o_spec,
        out_shape=jax.ShapeDtypeStruct((b, s, d), BF16),
        compiler_params=pltpu.CompilerParams(
            dimension_semantics=("arbitrary", "arbitrary"), vmem_limit_bytes=VMEM_LIMIT),
    )
    qpad =pltpu.VMEM((group, PAIR, 2 * blk), BF16)
    stat = pltpu.VMEM((group, 1, 2 * blk), F32)
    if lam4 is not None:
        return pl.pallas_call(
            functools.partial(_diff_attn_kernel, lambda_init=lambda_init),
            in_specs=[pl.BlockSpec(lam4.shape, lambda bi, h: (0, 0)),
                      pl.BlockSpec((PAIR, 1), lambda bi, h: (0, 0)),
                      qv_spec, k_spec, qv_spec],
            scratch_shapes=[qpad, stat, stat, pltpu.VMEM((group, PAIR, 2 * blk), F32)],
            name="diff_attention", **common,
        )(lam4, subln_g.reshape(PAIR, 1), qT, k, vT)
    return pl.pallas_call(
        _stick_attn_kernel,
        in_specs=[qv_spec, k_spec, qv_spec],
        scratch_shapes=[qpad, pltpu.VMEM((blk, blk), BF16), stat,
                        pltpu.VMEM((group, PAIR, blk), F32)],
        name="stick_attention", **common,
    )(qT, k, vT)


def _post_kernel(o_ref, x_ref, wo_ref, gpost_ref, gpre_ref, wgu_ref, wd_ref, gfpost_ref, out_ref,
                 *, ff_chunk):
    m = jnp.dot(o_ref[...], wo_ref[...], preferred_element_type=F32)
    x1 = x_ref[...] + _rms(m) * gpost_ref[...]
    hf = (_rms(x1) * gpre_ref[...]).astype(BF16)
    d_ff = wd_ref.shape[0]
    f = jnp.zeros(x1.shape, F32)
    for j in range(d_ff // ff_chunk):
        lo = j * ff_chunk
        gate = jnp.dot(hf, wgu_ref[:, lo:lo + ff_chunk], preferred_element_type=F32)
        up = jnp.dot(hf, wgu_ref[:, d_ff + lo:d_ff + lo + ff_chunk], preferred_element_type=F32)
        h = (gate * jax.nn.sigmoid(gate) * up).astype(BF16)
        f = f + jnp.dot(h, wd_ref[lo:lo + ff_chunk, :], preferred_element_type=F32)
    out_ref[...] = x1 + _rms(f) * gfpost_ref[...]


def _post(o, x, wo, gpost, gpre, wgu, wd, gfpost):
    n, d = x.shape
    d_ff = wd.shape[0]
    tm = POST_ROWS
    ff_chunk = d_ff // 2 if (d_ff // 2) % 128 == 0 else d_ff
    rows = lambda i: (i, 0)
    const = lambda i: (0, 0)
    resident = functools.partial(pl.BlockSpec, index_map=const, pipeline_mode=pl.Buffered(1))
    gain = pl.BlockSpec((1, d), const)
    return pl.pallas_call(
        functools.partial(_post_kernel, ff_chunk=ff_chunk),
        grid=(n // tm,),
        in_specs=[pl.BlockSpec((tm, d), rows), pl.BlockSpec((tm, d), rows),
                  resident((d, d)), gain, gain,
                  resident((d, 2 * d_ff)), resident((d_ff, d)), gain],
        out_specs=pl.BlockSpec((tm, d), rows),
        out_shape=jax.ShapeDtypeStruct((n, d), F32),
        compiler_params=pltpu.CompilerParams(
            dimension_semantics=("arbitrary",), vmem_limit_bytes=VMEM_LIMIT),
        name="post_swiglu",
    )(o, x, wo, gpost.reshape(1, d), gpre.reshape(1, d), wgu, wd, gfpost.reshape(1, d))


def _rope_tables(seq):
    pos = jnp.arange(seq, dtype=F32)
    inv_freq = ROPE_THETA ** (-jnp.arange(0, 2 * ROT_HALF, 2, dtype=F32) / (2 * ROT_HALF))
    ang = pos[:, None] * inv_freq[None, :]
    cos, sin = jnp.cos(ang), jnp.sin(ang)
    ones = jnp.ones((seq, HEAD_DIM - 2 * ROT_HALF), F32)
    zeros = jnp.zeros((seq, HEAD_DIM - 2 * ROT_HALF), F32)
    z8 = jnp.zeros_like(sin)
    ck = jnp.tile(jnp.concatenate([cos, cos, ones], axis=1), (1, 2))
    sk1 = jnp.tile(jnp.concatenate([-sin, z8, zeros], axis=1), (1, 2))
    sk2 = jnp.tile(jnp.concatenate([z8, sin, zeros], axis=1), (1, 2))
    return cos.T, sin.T, ck, sk1, sk2


def kernel(x, a_w_qkv, a_w_o, a_lambda_q1, a_lambda_k1, a_lambda_q2, a_lambda_k2, a_subln_g,
           kv_norm_g, kv_w, b_w_q, b_w_o, mix_pre_g, mix_post_g, ffn_pre_g, ffn_post_g,
           ffn_w_gate_up, ffn_w_down):
    b, s, d = x.shape
    assert a_w_qkv.shape[0] == 1 and b_w_q.shape[0] == 1 and mix_pre_g.shape[0] == 2
    assert s % PROJ_ROWS == 0 and (b * s) % POST_ROWS == 0 and d % (PAIR * ATT_GROUP) == 0

    w = a_w_qkv[0]
    q, k, vT = _project(x, mix_pre_g[0], mix_pre_g[0],
                        w[:, :d].T.astype(BF16), w[:, d:2 * d].astype(BF16),
                        w[:, 2 * d:].T.astype(BF16), rope_tables=_rope_tables(s))
    lam4 = jnp.stack([a_lambda_q1[0], a_lambda_k1[0], a_lambda_q2[0], a_lambda_k2[0]])
    lambda_init = LAMBDA_INIT_BASE - LAMBDA_INIT_AMP * math.exp(-LAMBDA_INIT_RATE * 0)
    o = _attention(q, k, vT, lam4=lam4, subln_g=a_subln_g[0], lambda_init=lambda_init)
    x = _post(o.reshape(b * s, d), x.reshape(b * s, d), a_w_o[0].astype(BF16), mix_post_g[0],
              ffn_pre_g[0], ffn_w_gate_up[0].astype(BF16), ffn_w_down[0].astype(BF16),
              ffn_post_g[0]).reshape(b, s, d)

    q, k, vT = _project(x, mix_pre_g[1], kv_norm_g,
                        b_w_q[0].T.astype(BF16), kv_w[:, :d].astype(BF16),
                        kv_w[:, d:].T.astype(BF16))
    o = _attention(q, k, vT)
    x = _post(o.reshape(b * s, d), x.reshape(b * s, d), b_w_o[0].astype(BF16), mix_post_g[1],
              ffn_pre_g[1], ffn_w_gate_up[1].astype(BF16), ffn_w_down[1].astype(BF16),
              ffn_post_g[1]).reshape(b, s, d)
    return x
```

```python
import functools
import math

import jax
import jax.numpy as jnp
from jax import lax
from jax.experimental import pallas as pl
from jax.experimental.pallas import tpu as pltpu

F32 = jnp.float32
BF16 = jnp.bfloat16

EPS = 1e-6
ROPE_THETA = 500000.0
ROT_HALF = 8
HEAD_DIM = 64
PAIR = 2 * HEAD_DIM
LAMBDA_INIT_BASE, LAMBDA_INIT_AMP, LAMBDA_INIT_RATE = 0.8, 0.6, 0.3
LOG2_E = math.log2(math.e)
MASKED_LOGIT = -1e30
NEGLIGIBLE_LOG2 = -150.0
ROW_CHUNK = 16

ATT_BLOCK = 256
ATT_GROUP = 4
PROJ_ROWS = 512
POST_ROWS = 512
VMEM_LIMIT = 56 * 1024 * 1024

_NT = (((1,), (1,)), ((), ()))


def _rms(x):
    return x * lax.rsqrt(jnp.mean(x * x, axis=-1, keepdims=True) + EPS)


def _proj_kernel(x_ref, gq_ref, gkv_ref, wqT_ref, wk_ref, wvT_ref, *rest, rope):
    if rope:
        cosT_ref, sinT_ref, ck_ref, sk1_ref, sk2_ref, qT_ref, k_ref, vT_ref = rest
    else:
        qT_ref, k_ref, vT_ref = rest
    tm = x_ref.shape[1]
    d = x_ref.shape[2]
    n_pairs = d // PAIR
    xn = _rms(x_ref[0])
    hq = (xn * gq_ref[...]).astype(BF16)
    hkv = (xn * gkv_ref[...]).astype(BF16)

    qT = lax.dot_general(wqT_ref[...], hq, _NT, preferred_element_type=F32)
    scale = HEAD_DIM ** -0.5 * LOG2_E
    for j in range(n_pairs):
        blk = qT[j * PAIR:(j + 1) * PAIR]
        if rope:
            cos, sin = cosT_ref[...], sinT_ref[...]
            parts = []
            for c in range(2):
                b = c * HEAD_DIM
                t1, t2 = blk[b:b + ROT_HALF], blk[b + ROT_HALF:b + 2 * ROT_HALF]
                parts += [t1 * cos - t2 * sin, t2 * cos + t1 * sin,
                          blk[b + 2 * ROT_HALF:b + HEAD_DIM]]
            blk = jnp.concatenate(parts, axis=0)
        blk = (blk * scale).astype(BF16)
        for t in range(tm // ATT_BLOCK):
            qT_ref[0, j, t] = blk[:, t * ATT_BLOCK:(t + 1) * ATT_BLOCK]

    kf = jnp.dot(hkv, wk_ref[...], preferred_element_type=F32)
    for j in range(n_pairs):
        kb = kf[:, j * PAIR:(j + 1) * PAIR]
        if rope:
            kb = (kb * ck_ref[...] + pltpu.roll(kb, PAIR - ROT_HALF, 1) * sk1_ref[...]
                  + pltpu.roll(kb, ROT_HALF, 1) * sk2_ref[...])
        k_ref[0, :, j * PAIR:(j + 1) * PAIR] = kb.astype(BF16)

    vT = lax.dot_general(wvT_ref[...], hkv, _NT, preferred_element_type=F32).astype(BF16)
    for j in range(n_pairs):
        for t in range(tm // ATT_BLOCK):
            vT_ref[0, j, t] = vT[j * PAIR:(j + 1) * PAIR, t * ATT_BLOCK:(t + 1) * ATT_BLOCK]


def _project(x, gq, gkv, wqT, wk, wvT, rope_tables=None):
    b, s, d = x.shape
    tm = PROJ_ROWS
    n_pairs = d // PAIR
    nblk = s // ATT_BLOCK
    per_step = tm // ATT_BLOCK
    const = lambda bi, i: (0, 0)
    in_specs = [
        pl.BlockSpec((1, tm, d), lambda bi, i: (bi, i, 0)),
        pl.BlockSpec((1, d), const),
        pl.BlockSpec((1, d), const),
        pl.BlockSpec((d, d), const),
        pl.BlockSpec((d, d), const),
        pl.BlockSpec((d, d), const),
    ]
    args = [x, gq.reshape(1, d), gkv.reshape(1, d), wqT, wk, wvT]
    if rope_tables is not None:
        cosT, sinT, ck, sk1, sk2 = rope_tables
        in_specs += [
            pl.BlockSpec((ROT_HALF, tm), lambda bi, i: (0, i)),
            pl.BlockSpec((ROT_HALF, tm), lambda bi, i: (0, i)),
            pl.BlockSpec((tm, PAIR), lambda bi, i: (i, 0)),
            pl.BlockSpec((tm, PAIR), lambda bi, i: (i, 0)),
            pl.BlockSpec((tm, PAIR), lambda bi, i: (i, 0)),
        ]
        args += [cosT, sinT, ck, sk1, sk2]
    t_spec = pl.BlockSpec((1, n_pairs, per_step, PAIR, ATT_BLOCK), lambda bi, i: (bi, 0, i, 0, 0))
    t_shape = jax.ShapeDtypeStruct((b, n_pairs, nblk, PAIR, ATT_BLOCK), BF16)
    return pl.pallas_call(
        functools.partial(_proj_kernel, rope=rope_tables is not None),
        grid=(b, s // tm),
        in_specs=in_specs,
        out_specs=[t_spec, pl.BlockSpec((1, tm, d), lambda bi, i: (bi, i, 0)), t_spec],
        out_shape=[t_shape, jax.ShapeDtypeStruct((b, s, d), BF16), t_shape],
        compiler_params=pltpu.CompilerParams(
            dimension_semantics=("arbitrary", "arbitrary"), vmem_limit_bytes=VMEM_LIMIT),
        name="proj_rope" if rope_tables is not None else "proj",
    )(*args)


def _run_step(calls):
    running = [stage(g, ki, edge) for stage, g, ki, edge in calls]
    for gen in running:
        next(gen)
    for gen in running:
        for _ in gen:
            raise AssertionError("a stage yields exactly once")


def _pipeline_fill(stages, group, ki):
    assert len(stages) - 1 <= group
    for t in range(group):
        _run_step([(stage, t - k, ki, True) for k, stage in enumerate(stages) if t - k >= 0])


def _pipeline_steady(stages, group, ki, ki_prev):
    for t in range(group):
        _run_step([(stage, (t - k) % group, ki if t >= k else ki_prev, False)
                   for k, stage in enumerate(stages)])


def _pipeline_drain(stages, group, ki):
    for t in range(len(stages) - 1):
        _run_step([(stage, group + t - k, ki, False) for k, stage in enumerate(stages) if t < k])


def _load_qpad(qT_ref, g, qi, qpad_sc):
    qT = qT_ref[0, g, qi].astype(F32)
    row = lax.broadcasted_iota(jnp.int32, qT.shape, 0)
    first = row < HEAD_DIM
    qpad_sc[g, :, :ATT_BLOCK] = jnp.where(first, qT, 0.0).astype(BF16)
    qpad_sc[g, :, ATT_BLOCK:] = jnp.where(first, 0.0, qT).astype(BF16)


def _block_positions():
    krow = lax.broadcasted_iota(jnp.int32, (ATT_BLOCK, 2 * ATT_BLOCK), 0)
    qcol = lax.broadcasted_iota(jnp.int32, (ATT_BLOCK, 2 * ATT_BLOCK), 1)
    qcol = jnp.where(qcol >= ATT_BLOCK, qcol - ATT_BLOCK, qcol)
    return krow, qcol


def _key_block(k_ref, g, ki):
    start = pl.multiple_of(ki * ATT_BLOCK, ATT_BLOCK)
    return k_ref[0, pl.ds(start, ATT_BLOCK), g * PAIR:(g + 1) * PAIR]


def _for_each_query_block(n_q, group, stages, begin, finish, older_keys_matter=None):
    def q_block(qi, carry):
        begin(qi)
        _pipeline_fill(stages, group, qi)

        if older_keys_matter is None:
            def pair(j, c):
                ki = qi - 1 - 2 * j
                _pipeline_steady(stages, group, ki, ki + 1)
                _pipeline_steady(stages, group, ki - 1, ki)
                return c

            def single(j, c):
                _pipeline_steady(stages, group, 0, 1)
                return c
            lax.fori_loop(0, qi // 2, pair, 0)
            lax.fori_loop(0, qi % 2, single, 0)
            last = 0
        else:
            def steady(carry):
                j, _ = carry
                ki = qi - 1 - j
                _pipeline_steady(stages, group, ki, ki + 1)
                return j + 1, older_keys_matter()
            done, _ = lax.while_loop(lambda c: jnp.logical_and(c[0] < qi, c[1]), steady,
                                     (jnp.int32(0), True))
            last = qi - done
        _pipeline_drain(stages, group, last)
        finish(pl.ds(pl.multiple_of(qi * ATT_BLOCK, ATT_BLOCK), ATT_BLOCK))
        return carry
    lax.fori_loop(0, n_q, q_block, 0)


def _diff_attn_kernel(lam_ref, g_ref, qT_ref, k_ref, vT_ref, o_ref,
                      qpad_sc, s_sc, p_sc, alpha_sc, m_sc, l_sc, acc_sc, *, lambda_init):
    blk = ATT_BLOCK
    group, n_q = qT_ref.shape[1], qT_ref.shape[2]
    lam4 = lam_ref[...]
    lam = (jnp.exp(jnp.sum(lam4[0:1] * lam4[1:2], axis=1, keepdims=True))
           - jnp.exp(jnp.sum(lam4[2:3] * lam4[3:4], axis=1, keepdims=True)) + lambda_init)

    def scores(g, ki, causal_edge):
        s = jnp.dot(_key_block(k_ref, g, ki), qpad_sc[g], preferred_element_type=F32)
        if causal_edge:
            krow, qcol = _block_positions()
            s = jnp.where(krow <= qcol, s, -jnp.inf)
        s_sc[g] = s
        yield

    def running_max(g, ki, causal_edge):
        top = s_sc[g, :ROW_CHUNK]
        for r in range(ROW_CHUNK, blk, ROW_CHUNK):
            top = jnp.maximum(top, s_sc[g, r:r + ROW_CHUNK])
        m_prev = m_sc[g]
        m_new = jnp.maximum(m_prev, jnp.max(top, axis=0, keepdims=True))
        alpha_sc[g] = jnp.exp2(m_prev - m_new)
        m_sc[g] = m_new
        yield

    def exponentials(g, ki, causal_edge):
        m = jnp.broadcast_to(m_sc[g], (ROW_CHUNK, 2 * blk))
        total = jnp.zeros((ROW_CHUNK, 2 * blk), F32)
        for r in range(0, blk, ROW_CHUNK):
            rows = slice(r, r + ROW_CHUNK)
            p = jnp.exp2(s_sc[g, rows] - m)
            total = total + p
            p_sc[g, rows] = p.astype(BF16)
        l_sc[g] = alpha_sc[g] * l_sc[g] + jnp.sum(total, axis=0, keepdims=True)
        yield

    def accumulate(g, ki, causal_edge):
        acc_sc[g] = alpha_sc[g] * acc_sc[g] + jnp.dot(
            vT_ref[0, g, ki], p_sc[g], preferred_element_type=F32)
        yield

    def begin(qi):
        for g in range(group):
            _load_qpad(qT_ref, g, qi, qpad_sc)
        m_sc[...] = jnp.full(m_sc.shape, -jnp.inf, F32)
        l_sc[...] = jnp.zeros(l_sc.shape, F32)
        acc_sc[...] = jnp.zeros(acc_sc.shape, F32)

    def finish(rows):
        for g in range(group):
            acc, l = acc_sc[g], l_sc[g]
            o = acc[:, :blk] / l[:, :blk] - lam * (acc[:, blk:] / l[:, blk:])
            o = o * lax.rsqrt(jnp.mean(o * o, axis=0, keepdims=True) + EPS)
            o = o * g_ref[...] * (1.0 - lambda_init)
            o_ref[0, rows, g * PAIR:(g + 1) * PAIR] = o.T.astype(BF16)

    _for_each_query_block(n_q, group, [scores, running_max, exponentials, accumulate],
                          begin, finish)


def _stick_attn_kernel(qT_ref, k_ref, vT_ref, o_ref, qpad_sc, u_sc, z_sc, lb_sc, h_sc, row0_sc,
                       btw_sc, a_sc, c_sc, acc_sc):
    blk = ATT_BLOCK
    group, n_q = qT_ref.shape[1], qT_ref.shape[2]
    r = lax.broadcasted_iota(jnp.int32, (blk, blk), 0)
    c = lax.broadcasted_iota(jnp.int32, (blk, blk), 1)
    u_sc[...] = jnp.where(c > r, 1.0, 0.0).astype(BF16)

    def scores(g, ki, causal_edge):
        z = jnp.dot(_key_block(k_ref, g, ki), qpad_sc[g], preferred_element_type=F32)
        if causal_edge:
            krow, qcol = _block_positions()
            z = jnp.where(krow < qcol, z, MASKED_LOGIT)
        z_sc[g] = z
        yield

    def log_sigmoids(g, ki, causal_edge):
        for r in range(0, blk, ROW_CHUNK):
            rows = slice(r, r + ROW_CHUNK)
            z = z_sc[g, rows]
            neg_abs = lax.bitcast_convert_type(
                lax.bitcast_convert_type(z, jnp.int32) | jnp.int32(-2 ** 31), F32)
            log_denom = jnp.log(1.0 + jnp.exp2(neg_abs)) * LOG2_E
            log_beta = jnp.minimum(z, 0.0) - log_denom
            log_1m = log_beta - z
            lb_sc[g, rows] = log_beta
            h_sc[g, rows] = log_1m.astype(BF16)
            if r == 0:
                row0_sc[g] = log_1m[0:1]
        yield

    def suffix_sums(g, ki, causal_edge):
        btw_sc[g] = jnp.dot(u_sc[...], h_sc[g], preferred_element_type=F32)
        yield

    def weights(g, ki, causal_edge):
        c_prev = c_sc[g]
        c_rows = jnp.broadcast_to(c_prev, (ROW_CHUNK, 2 * blk))
        for r in range(0, blk, ROW_CHUNK):
            rows = slice(r, r + ROW_CHUNK)
            a_sc[g, rows] = jnp.exp2(lb_sc[g, rows] + btw_sc[g, rows] + c_rows).astype(BF16)
        c_sc[g] = c_prev + btw_sc[g, 0:1] + row0_sc[g]
        yield

    def accumulate(g, ki, causal_edge):
        vT = vT_ref[0, g, ki]
        a = a_sc[g]
        acc_sc[g, :HEAD_DIM] += jnp.dot(vT[:HEAD_DIM], a[:, :blk], preferred_element_type=F32)
        acc_sc[g, HEAD_DIM:] += jnp.dot(vT[HEAD_DIM:], a[:, blk:], preferred_element_type=F32)
        yield

    def begin(qi):
        for g in range(group):
            _load_qpad(qT_ref, g, qi, qpad_sc)
        c_sc[...] = jnp.zeros(c_sc.shape, F32)
        acc_sc[...] = jnp.zeros(acc_sc.shape, F32)

    def finish(rows):
        for g in range(group):
            o_ref[0, rows, g * PAIR:(g + 1) * PAIR] = acc_sc[g].T.astype(BF16)

    def older_keys_matter():
        return jnp.max(c_sc[...]) > NEGLIGIBLE_LOG2

    _for_each_query_block(n_q, group, [scores, log_sigmoids, suffix_sums, weights, accumulate],
                          begin, finish, older_keys_matter)


def _attention(qT, k, vT, *, lam4=None, subln_g=None, lambda_init=None):
    b, n_pairs, nblk, _, blk = qT.shape
    s = nblk * blk
    d = k.shape[-1]
    group = ATT_GROUP
    qv_spec = pl.BlockSpec((1, group, nblk, PAIR, blk), lambda bi, h: (bi, h, 0, 0, 0))
    k_spec = pl.BlockSpec((1, s, group * PAIR), lambda bi, h: (bi, 0, h))
    o_spec = pl.BlockSpec((1, s, group * PAIR), lambda bi, h: (bi, 0, h))
    common = dict(
        grid=(b, n_pairs // group),
        out_specs=o_spec,
        out_shape=jax.ShapeDtypeStruct((b, s, d), BF16),
        compiler_params=pltpu.CompilerParams(
            dimension_semantics=("arbitrary", "arbitrary"), vmem_limit_bytes=VMEM_LIMIT),
    )
    qpad = pltpu.VMEM((group, PAIR, 2 * blk), BF16)
    stat = pltpu.VMEM((group, 1, 2 * blk), F32)
    tile_f32 = pltpu.VMEM((group, blk, 2 * blk), F32)
    tile_bf16 = pltpu.VMEM((group, blk, 2 * blk), BF16)
    if lam4 is not None:
        return pl.pallas_call(
            functools.partial(_diff_attn_kernel, lambda_init=lambda_init),
            in_specs=[pl.BlockSpec(lam4.shape, lambda bi, h: (0, 0)),
                      pl.BlockSpec((PAIR, 1), lambda bi, h: (0, 0)),
                      qv_spec, k_spec, qv_spec],
            scratch_shapes=[qpad, tile_f32, tile_bf16, stat, stat, stat,
                            pltpu.VMEM((group, PAIR, 2 * blk), F32)],
            name="diff_attention", **common,
        )(lam4, subln_g.reshape(PAIR, 1), qT, k, vT)
    return pl.pallas_call(
        _stick_attn_kernel,
        in_specs=[qv_spec, k_spec, qv_spec],
        scratch_shapes=[qpad, pltpu.VMEM((blk, blk), BF16), tile_f32, tile_f32,
                        tile_bf16, stat, tile_f32, tile_bf16,
                        stat, pltpu.VMEM((group, PAIR, blk), F32)],
        name="stick_attention", **common,
    )(qT, k, vT)


def _post_kernel(o_ref, x_ref, wo_ref, gpost_ref, gpre_ref, wgu_ref, wd_ref, gfpost_ref, out_ref,
                 *, ff_chunk):
    m = jnp.dot(o_ref[...], wo_ref[...], preferred_element_type=F32)
    x1 = x_ref[...] + _rms(m) * gpost_ref[...]
    hf = (_rms(x1) * gpre_ref[...]).astype(BF16)
    d_ff = wd_ref.shape[0]
    f = jnp.zeros(x1.shape, F32)
    for j in range(d_ff // ff_chunk):
        lo = j * ff_chunk
        gate = jnp.dot(hf, wgu_ref[:, lo:lo + ff_chunk], preferred_element_type=F32)
        up = jnp.dot(hf, wgu_ref[:, d_ff + lo:d_ff + lo + ff_chunk], preferred_element_type=F32)
        h = (gate * jax.nn.sigmoid(gate) * up).astype(BF16)
        f = f + jnp.dot(h, wd_ref[lo:lo + ff_chunk, :], preferred_element_type=F32)
    out_ref[...] = x1 + _rms(f) * gfpost_ref[...]


def _post(o, x, wo, gpost, gpre, wgu, wd, gfpost):
    n, d = x.shape
    d_ff = wd.shape[0]
    tm = POST_ROWS
    ff_chunk = d_ff // 2 if (d_ff // 2) % 128 == 0 else d_ff
    rows = lambda i: (i, 0)
    const = lambda i: (0, 0)
    resident = functools.partial(pl.BlockSpec, index_map=const, pipeline_mode=pl.Buffered(1))
    gain = pl.BlockSpec((1, d), const)
    return pl.pallas_call(
        functools.partial(_post_kernel, ff_chunk=ff_chunk),
        grid=(n // tm,),
        in_specs=[pl.BlockSpec((tm, d), rows), pl.BlockSpec((tm, d), rows),
                  resident((d, d)), gain, gain,
                  resident((d, 2 * d_ff)), resident((d_ff, d)), gain],
        out_specs=pl.BlockSpec((tm, d), rows),
        out_shape=jax.ShapeDtypeStruct((n, d), F32),
        compiler_params=pltpu.CompilerParams(
            dimension_semantics=("arbitrary",), vmem_limit_bytes=VMEM_LIMIT),
        name="post_swiglu",
    )(o, x, wo, gpost.reshape(1, d), gpre.reshape(1, d), wgu, wd, gfpost.reshape(1, d))


def _rope_tables(seq):
    pos = jnp.arange(seq, dtype=F32)
    inv_freq = ROPE_THETA ** (-jnp.arange(0, 2 * ROT_HALF, 2, dtype=F32) / (2 * ROT_HALF))
    ang = pos[:, None] * inv_freq[None, :]
    cos, sin = jnp.cos(ang), jnp.sin(ang)
    ones = jnp.ones((seq, HEAD_DIM - 2 * ROT_HALF), F32)
    zeros = jnp.zeros((seq, HEAD_DIM - 2 * ROT_HALF), F32)
    z8 = jnp.zeros_like(sin)
    ck = jnp.tile(jnp.concatenate([cos, cos, ones], axis=1), (1, 2))
    sk1 = jnp.tile(jnp.concatenate([-sin, z8, zeros], axis=1), (1, 2))
    sk2 = jnp.tile(jnp.concatenate([z8, sin, zeros], axis=1), (1, 2))
    return cos.T, sin.T, ck, sk1, sk2


def kernel(x, a_w_qkv, a_w_o, a_lambda_q1, a_lambda_k1, a_lambda_q2, a_lambda_k2, a_subln_g,
           kv_norm_g, kv_w, b_w_q, b_w_o, mix_pre_g, mix_post_g, ffn_pre_g, ffn_post_g,
           ffn_w_gate_up, ffn_w_down):
    b, s, d = x.shape
    assert a_w_qkv.shape[0] == 1 and b_w_q.shape[0] == 1 and mix_pre_g.shape[0] == 2
    assert s % PROJ_ROWS == 0 and (b * s) % POST_ROWS == 0 and d % (PAIR * ATT_GROUP) == 0

    w = a_w_qkv[0]
    q, k, vT = _project(x, mix_pre_g[0], mix_pre_g[0],
                        w[:, :d].T.astype(BF16), w[:, d:2 * d].astype(BF16),
                        w[:, 2 * d:].T.astype(BF16), rope_tables=_rope_tables(s))
    lam4 = jnp.stack([a_lambda_q1[0], a_lambda_k1[0], a_lambda_q2[0], a_lambda_k2[0]])
    lambda_init = LAMBDA_INIT_BASE - LAMBDA_INIT_AMP * math.exp(-LAMBDA_INIT_RATE * 0)
    o = _attention(q, k, vT, lam4=lam4, subln_g=a_subln_g[0], lambda_init=lambda_init)
    x = _post(o.reshape(b * s, d), x.reshape(b * s, d), a_w_o[0].astype(BF16), mix_post_g[0],
              ffn_pre_g[0], ffn_w_gate_up[0].astype(BF16), ffn_w_down[0].astype(BF16),
              ffn_post_g[0]).reshape(b, s, d)

    q, k, vT = _project(x, mix_pre_g[1], kv_norm_g,
                        b_w_q[0].T.astype(BF16), kv_w[:, :d].astype(BF16),
                        kv_w[:, d:].T.astype(BF16))
    o = _attention(q, k, vT)
    x = _post(o.reshape(b * s, d), x.reshape(b * s, d), b_w_o[0].astype(BF16), mix_post_g[1],
              ffn_pre_g[1], ffn_w_gate_up[1].astype(BF16), ffn_w_down[1].astype(BF16),
              ffn_post_g[1]).reshape(b, s, d)
    return x
```

```python
import functools
import math

import jax
import jax.numpy as jnp
from jax import lax
from jax.experimental import pallas as pl
from jax.experimental.pallas import tpu as pltpu

F32 = jnp.float32
BF16 = jnp.bfloat16

EPS = 1e-6
ROPE_THETA = 500000.0
ROT_HALF = 8
HEAD_DIM = 64
PAIR = 2 * HEAD_DIM
LAMBDA_INIT_BASE, LAMBDA_INIT_AMP, LAMBDA_INIT_RATE = 0.8, 0.6, 0.3
LOG2_E = math.log2(math.e)
MASKED_LOGIT = -1e30
NEGLIGIBLE_LOG2 = -150.0

ATT_BLOCK = 256
ATT_GROUP = 4
PROJ_ROWS = 512
POST_ROWS = 512
VMEM_LIMIT = 56 * 1024 * 1024

_NT = (((1,), (1,)), ((), ()))


def _rms(x):
    return x * lax.rsqrt(jnp.mean(x * x, axis=-1, keepdims=True) + EPS)


def _proj_kernel(x_ref, gq_ref, gkv_ref, wqT_ref, wk_ref, wvT_ref, *rest, rope):
    if rope:
        cosT_ref, sinT_ref, ck_ref, sk1_ref, sk2_ref, qT_ref, k_ref, vT_ref = rest
    else:
        qT_ref, k_ref, vT_ref = rest
    tm = x_ref.shape[1]
    d = x_ref.shape[2]
    n_pairs = d // PAIR
    xn = _rms(x_ref[0])
    hq = (xn * gq_ref[...]).astype(BF16)
    hkv = (xn * gkv_ref[...]).astype(BF16)

    qT = lax.dot_general(wqT_ref[...], hq, _NT, preferred_element_type=F32)
    scale = HEAD_DIM ** -0.5 * LOG2_E
    for j in range(n_pairs):
        blk = qT[j * PAIR:(j + 1) * PAIR]
        if rope:
            cos, sin = cosT_ref[...], sinT_ref[...]
            parts = []
            for c in range(2):
                b = c * HEAD_DIM
                t1, t2 = blk[b:b + ROT_HALF], blk[b + ROT_HALF:b + 2 * ROT_HALF]
                parts += [t1 * cos - t2 * sin, t2 * cos + t1 * sin,
                          blk[b + 2 * ROT_HALF:b + HEAD_DIM]]
            blk = jnp.concatenate(parts, axis=0)
        blk = (blk * scale).astype(BF16)
        for t in range(tm // ATT_BLOCK):
            qT_ref[0, j, t] = blk[:, t * ATT_BLOCK:(t + 1) * ATT_BLOCK]

    kf = jnp.dot(hkv, wk_ref[...], preferred_element_type=F32)
    for j in range(n_pairs):
        kb = kf[:, j * PAIR:(j + 1) * PAIR]
        if rope:
            kb = (kb * ck_ref[...] + pltpu.roll(kb, PAIR - ROT_HALF, 1) * sk1_ref[...]
                  + pltpu.roll(kb, ROT_HALF, 1) * sk2_ref[...])
        k_ref[0, :, j * PAIR:(j + 1) * PAIR] = kb.astype(BF16)

    vT = lax.dot_general(wvT_ref[...], hkv, _NT, preferred_element_type=F32).astype(BF16)
    for j in range(n_pairs):
        for t in range(tm // ATT_BLOCK):
            vT_ref[0, j, t] = vT[j * PAIR:(j + 1) * PAIR, t * ATT_BLOCK:(t + 1) * ATT_BLOCK]


def _project(x, gq, gkv, wqT, wk, wvT, rope_tables=None):
    b, s, d = x.shape
    tm = PROJ_ROWS
    n_pairs = d // PAIR
    nblk = s // ATT_BLOCK
    per_step = tm // ATT_BLOCK
    const = lambda bi, i: (0, 0)
    in_specs = [
        pl.BlockSpec((1, tm, d), lambda bi, i: (bi, i, 0)),
        pl.BlockSpec((1, d), const),
        pl.BlockSpec((1, d), const),
        pl.BlockSpec((d, d), const),
        pl.BlockSpec((d, d), const),
        pl.BlockSpec((d, d), const),
    ]
    args = [x, gq.reshape(1, d), gkv.reshape(1, d), wqT, wk, wvT]
    if rope_tables is not None:
        cosT, sinT, ck, sk1, sk2 = rope_tables
        in_specs += [
            pl.BlockSpec((ROT_HALF, tm), lambda bi, i: (0, i)),
            pl.BlockSpec((ROT_HALF, tm), lambda bi, i: (0, i)),
            pl.BlockSpec((tm, PAIR), lambda bi, i: (i, 0)),
            pl.BlockSpec((tm, PAIR), lambda bi, i: (i, 0)),
            pl.BlockSpec((tm, PAIR), lambda bi, i: (i, 0)),
        ]
        args += [cosT, sinT, ck, sk1, sk2]
    t_spec = pl.BlockSpec((1, n_pairs, per_step, PAIR, ATT_BLOCK), lambda bi, i: (bi, 0, i, 0, 0))
    t_shape = jax.ShapeDtypeStruct((b, n_pairs, nblk, PAIR, ATT_BLOCK), BF16)
    return pl.pallas_call(
        functools.partial(_proj_kernel, rope=rope_tables is not None),
        grid=(b, s // tm),
        in_specs=in_specs,
        out_specs=[t_spec, pl.BlockSpec((1, tm, d), lambda bi, i: (bi, i, 0)), t_spec],
        out_shape=[t_shape, jax.ShapeDtypeStruct((b, s, d), BF16), t_shape],
        compiler_params=pltpu.CompilerParams(
            dimension_semantics=("arbitrary", "arbitrary"), vmem_limit_bytes=VMEM_LIMIT),
        name="proj_rope" if rope_tables is not None else "proj",
    )(*args)


def _run_step(calls):
    running = [stage(g, ki, edge) for stage, g, ki, edge in calls]
    for gen in running:
        next(gen)
    for gen in running:
        for _ in gen:
            raise AssertionError("a stage yields exactly once")


def _pipeline_fill(stages, group, ki):
    assert len(stages) - 1 <= group
    for t in range(group):
        _run_step([(stage, t - k, ki, True) for k, stage in enumerate(stages) if t - k >= 0])


def _pipeline_steady(stages, group, ki, ki_prev):
    for t in range(group):
        _run_step([(stage, (t - k) % group, ki if t >= k else ki_prev, False)
                   for k, stage in enumerate(stages)])


def _pipeline_drain(stages, group, ki):
    for t in range(len(stages) - 1):
        _run_step([(stage, group + t - k, ki, False) for k, stage in enumerate(stages) if t < k])


def _load_qpad(qT_ref, g, qi, qpad_sc):
    qT = qT_ref[0, g, qi].astype(F32)
    row = lax.broadcasted_iota(jnp.int32, qT.shape, 0)
    first = row < HEAD_DIM
    qpad_sc[g, :, :ATT_BLOCK] = jnp.where(first, qT, 0.0).astype(BF16)
    qpad_sc[g, :, ATT_BLOCK:] = jnp.where(first, 0.0, qT).astype(BF16)


def _block_positions():
    krow = lax.broadcasted_iota(jnp.int32, (ATT_BLOCK, 2 * ATT_BLOCK), 0)
    qcol = lax.broadcasted_iota(jnp.int32, (ATT_BLOCK, 2 * ATT_BLOCK), 1)
    qcol = jnp.where(qcol >= ATT_BLOCK, qcol - ATT_BLOCK, qcol)
    return krow, qcol


def _key_block(k_ref, g, ki):
    start = pl.multiple_of(ki * ATT_BLOCK, ATT_BLOCK)
    return k_ref[0, pl.ds(start, ATT_BLOCK), g * PAIR:(g + 1) * PAIR]


def _for_each_query_block(n_q, group, stages, begin, finish, older_keys_matter=None):
    def q_block(qi, carry):
        begin(qi)
        _pipeline_fill(stages, group, qi)

        if older_keys_matter is None:
            def pair(j, c):
                ki = qi - 1 - 2 * j
                _pipeline_steady(stages, group, ki, ki + 1)
                _pipeline_steady(stages, group, ki - 1, ki)
                return c

            def single(j, c):
                _pipeline_steady(stages, group, 0, 1)
                return c
            lax.fori_loop(0, qi // 2, pair, 0)
            lax.fori_loop(0, qi % 2, single, 0)
            last = 0
        else:
            def steady(carry):
                j, _ = carry
                ki = qi - 1 - j
                _pipeline_steady(stages, group, ki, ki + 1)
                return j + 1, older_keys_matter()
            done, _ = lax.while_loop(lambda c: jnp.logical_and(c[0] < qi, c[1]), steady,
                                     (jnp.int32(0), True))
            last = qi - done
        _pipeline_drain(stages, group, last)
        finish(pl.ds(pl.multiple_of(qi * ATT_BLOCK, ATT_BLOCK), ATT_BLOCK))
        return carry
    lax.fori_loop(0, n_q, q_block, 0)


def _diff_attn_kernel(lam_ref, g_ref, qT_ref, k_ref, vT_ref, o_ref,
                      qpad_sc, s_sc, p_sc, alpha_sc, m_sc, l_sc, acc_sc, *, lambda_init):
    blk = ATT_BLOCK
    group, n_q = qT_ref.shape[1], qT_ref.shape[2]
    lam4 = lam_ref[...]
    lam = (jnp.exp(jnp.sum(lam4[0:1] * lam4[1:2], axis=1, keepdims=True))
           - jnp.exp(jnp.sum(lam4[2:3] * lam4[3:4], axis=1, keepdims=True)) + lambda_init)

    def scores(g, ki, causal_edge):
        s = jnp.dot(_key_block(k_ref, g, ki), qpad_sc[g], preferred_element_type=F32)
        if causal_edge:
            krow, qcol = _block_positions()
            s = jnp.where(krow <= qcol, s, -jnp.inf)
        s_sc[g] = s
        m_prev = m_sc[g]
        m_new = jnp.maximum(m_prev, jnp.max(s, axis=0, keepdims=True))
        alpha_sc[g] = jnp.exp2(m_prev - m_new)
        m_sc[g] = m_new
        yield

    def exponentials(g, ki, causal_edge):
        p = jnp.exp2(s_sc[g] - m_sc[g])
        l_sc[g] = alpha_sc[g] * l_sc[g] + jnp.sum(p, axis=0, keepdims=True)
        p_sc[g] = p.astype(BF16)
        yield

    def accumulate(g, ki, causal_edge):
        acc_sc[g] = alpha_sc[g] * acc_sc[g] + jnp.dot(
            vT_ref[0, g, ki], p_sc[g], preferred_element_type=F32)
        yield

    def begin(qi):
        for g in range(group):
            _load_qpad(qT_ref, g, qi, qpad_sc)
        m_sc[...] = jnp.full(m_sc.shape, -jnp.inf, F32)
        l_sc[...] = jnp.zeros(l_sc.shape, F32)
        acc_sc[...] = jnp.zeros(acc_sc.shape, F32)

    def finish(rows):
        for g in range(group):
            acc, l = acc_sc[g], l_sc[g]
            o = acc[:, :blk] / l[:, :blk] - lam * (acc[:, blk:] / l[:, blk:])
            o = o * lax.rsqrt(jnp.mean(o * o, axis=0, keepdims=True) + EPS)
            o = o * g_ref[...] * (1.0 - lambda_init)
            o_ref[0, rows, g * PAIR:(g + 1) * PAIR] = o.T.astype(BF16)

    _for_each_query_block(n_q, group, [scores, exponentials, accumulate], begin, finish)


def _stick_attn_kernel(qT_ref, k_ref, vT_ref, o_ref, qpad_sc, u_sc, z_sc, lb_sc, h_sc, row0_sc,
                       btw_sc, a_sc, c_sc, acc_sc):
    blk = ATT_BLOCK
    group, n_q = qT_ref.shape[1], qT_ref.shape[2]
    r = lax.broadcasted_iota(jnp.int32, (blk, blk), 0)
    c = lax.broadcasted_iota(jnp.int32, (blk, blk), 1)
    u_sc[...] = jnp.where(c > r, 1.0, 0.0).astype(BF16)

    def scores(g, ki, causal_edge):
        z = jnp.dot(_key_block(k_ref, g, ki), qpad_sc[g], preferred_element_type=F32)
        if causal_edge:
            krow, qcol = _block_positions()
            z = jnp.where(krow < qcol, z, MASKED_LOGIT)
        z_sc[g] = z
        yield

    def log_sigmoids(g, ki, causal_edge):
        z = z_sc[g]
        log_denom = jnp.log(1.0 + jnp.exp2(-jnp.abs(z))) * LOG2_E
        log_beta = jnp.minimum(z, 0.0) - log_denom
        log_1m = log_beta - z
        lb_sc[g] = log_beta
        h_sc[g] = log_1m.astype(BF16)
        row0_sc[g] = log_1m[0:1]
        yield

    def suffix_sums(g, ki, causal_edge):
        btw_sc[g] = jnp.dot(u_sc[...], h_sc[g], preferred_element_type=F32)
        yield

    def weights(g, ki, causal_edge):
        between, c_prev = btw_sc[g], c_sc[g]
        a_sc[g] = jnp.exp2(lb_sc[g] + between + c_prev).astype(BF16)
        c_sc[g] = c_prev + between[0:1] + row0_sc[g]
        yield

    def accumulate(g, ki, causal_edge):
        vT = vT_ref[0, g, ki]
        a = a_sc[g]
        acc_sc[g, :HEAD_DIM] += jnp.dot(vT[:HEAD_DIM], a[:, :blk], preferred_element_type=F32)
        acc_sc[g, HEAD_DIM:] += jnp.dot(vT[HEAD_DIM:], a[:, blk:], preferred_element_type=F32)
        yield

    def begin(qi):
        for g in range(group):
            _load_qpad(qT_ref, g, qi, qpad_sc)
        c_sc[...] = jnp.zeros(c_sc.shape, F32)
        acc_sc[...] = jnp.zeros(acc_sc.shape, F32)

    def finish(rows):
        for g in range(group):
            o_ref[0, rows, g * PAIR:(g + 1) * PAIR] = acc_sc[g].T.astype(BF16)

    def older_keys_matter():
        return jnp.max(c_sc[...]) > NEGLIGIBLE_LOG2

    _for_each_query_block(n_q, group, [scores, log_sigmoids, suffix_sums, weights, accumulate],
                          begin, finish, older_keys_matter)


def _attention(qT, k, vT, *, lam4=None, subln_g=None, lambda_init=None):
    b, n_pairs, nblk, _, blk = qT.shape
    s = nblk * blk
    d = k.shape[-1]
    group = ATT_GROUP
    qv_spec = pl.BlockSpec((1, group, nblk, PAIR, blk), lambda bi, h: (bi, h, 0, 0, 0))
    k_spec = pl.BlockSpec((1, s, group * PAIR), lambda bi, h: (bi, 0, h))
    o_spec = pl.BlockSpec((1, s, group * PAIR), lambda bi, h: (bi, 0, h))
    common = dict(
        grid=(b, n_pairs // group),
        out_specs=o_spec,
        out_shape=jax.ShapeDtypeStruct((b, s, d), BF16),
        compiler_params=pltpu.CompilerParams(
            dimension_semantics=("arbitrary", "arbitrary"), vmem_limit_bytes=VMEM_LIMIT),
    )
    qpad =pltpu.VMEM((group, PAIR, 2 * blk), BF16)
    stat = pltpu.VMEM((group, 1, 2 * blk), F32)
    tile_f32 = pltpu.VMEM((group, blk, 2 * blk), F32)
    tile_bf16 = pltpu.VMEM((group, blk, 2 * blk), BF16)
    if lam4 is not None:
        return pl.pallas_call(
            functools.partial(_diff_attn_kernel, lambda_init=lambda_init),
            in_specs=[pl.BlockSpec(lam4.shape, lambda bi, h: (0, 0)),
                      pl.BlockSpec((PAIR, 1), lambda bi, h: (0, 0)),
                      qv_spec, k_spec, qv_spec],
            scratch_shapes=[qpad, tile_f32, tile_bf16, stat, stat, stat,
                            pltpu.VMEM((group, PAIR, 2 * blk), F32)],
            name="diff_attention", **common,
        )(lam4, subln_g.reshape(PAIR, 1), qT, k, vT)
    return pl.pallas_call(
        _stick_attn_kernel,
        in_specs=[qv_spec, k_spec, qv_spec],
        scratch_shapes=[qpad, pltpu.VMEM((blk, blk), BF16), tile_f32, tile_f32, tile_bf16, stat,
                        tile_f32, tile_bf16, stat, pltpu.VMEM((group, PAIR, blk), F32)],
        name="stick_attention", **common,
    )(qT, k, vT)


def _post_kernel(o_ref, x_ref, wo_ref, gpost_ref, gpre_ref, wgu_ref, wd_ref, gfpost_ref, out_ref,
                 *, ff_chunk):
    m = jnp.dot(o_ref[...], wo_ref[...], preferred_element_type=F32)
    x1 = x_ref[...] + _rms(m) * gpost_ref[...]
    hf = (_rms(x1) * gpre_ref[...]).astype(BF16)
    d_ff = wd_ref.shape[0]
    f = jnp.zeros(x1.shape, F32)
    for j in range(d_ff // ff_chunk):
        lo = j * ff_chunk
        gate = jnp.dot(hf, wgu_ref[:, lo:lo + ff_chunk], preferred_element_type=F32)
        up = jnp.dot(hf, wgu_ref[:, d_ff + lo:d_ff + lo + ff_chunk], preferred_element_type=F32)
        h = (gate * jax.nn.sigmoid(gate) * up).astype(BF16)
        f = f + jnp.dot(h, wd_ref[lo:lo + ff_chunk, :], preferred_element_type=F32)
    out_ref[...] = x1 + _rms(f) * gfpost_ref[...]


def _post(o, x, wo, gpost, gpre, wgu, wd, gfpost):
    n, d = x.shape
    d_ff = wd.shape[0]
    tm = POST_ROWS
    ff_chunk = d_ff // 2 if (d_ff // 2) % 128 == 0 else d_ff
    rows = lambda i: (i, 0)
    const = lambda i: (0, 0)
    resident = functools.partial(pl.BlockSpec, index_map=const, pipeline_mode=pl.Buffered(1))
    gain = pl.BlockSpec((1, d), const)
    return pl.pallas_call(
        functools.partial(_post_kernel, ff_chunk=ff_chunk),
        grid=(n // tm,),
        in_specs=[pl.BlockSpec((tm, d), rows), pl.BlockSpec((tm, d), rows),
                  resident((d, d)), gain, gain,
                  resident((d, 2 * d_ff)), resident((d_ff, d)), gain],
        out_specs=pl.BlockSpec((tm, d), rows),
        out_shape=jax.ShapeDtypeStruct((n, d), F32),
        compiler_params=pltpu.CompilerParams(
            dimension_semantics=("arbitrary",), vmem_limit_bytes=VMEM_LIMIT),
        name="post_swiglu",
    )(o, x, wo, gpost.reshape(1, d), gpre.reshape(1, d), wgu, wd, gfpost.reshape(1, d))


def _rope_tables(seq):
    pos = jnp.arange(seq, dtype=F32)
    inv_freq = ROPE_THETA ** (-jnp.arange(0, 2 * ROT_HALF, 2, dtype=F32) / (2 * ROT_HALF))
    ang = pos[:, None] * inv_freq[None, :]
    cos, sin = jnp.cos(ang), jnp.sin(ang)
    ones = jnp.ones((seq, HEAD_DIM - 2 * ROT_HALF), F32)
    zeros = jnp.zeros((seq, HEAD_DIM - 2 * ROT_HALF), F32)
    z8 = jnp.zeros_like(sin)
    ck = jnp.tile(jnp.concatenate([cos, cos, ones], axis=1), (1, 2))
    sk1 = jnp.tile(jnp.concatenate([-sin, z8, zeros], axis=1), (1, 2))
    sk2 = jnp.tile(jnp.concatenate([z8, sin, zeros], axis=1), (1, 2))
    return cos.T, sin.T, ck, sk1, sk2


def kernel(x, a_w_qkv, a_w_o, a_lambda_q1, a_lambda_k1, a_lambda_q2, a_lambda_k2, a_subln_g,
           kv_norm_g, kv_w, b_w_q, b_w_o, mix_pre_g, mix_post_g, ffn_pre_g, ffn_post_g,
           ffn_w_gate_up, ffn_w_down):
    b, s, d = x.shape
    assert a_w_qkv.shape[0] == 1 and b_w_q.shape[0] == 1 and mix_pre_g.shape[0] == 2
    assert s % PROJ_ROWS == 0 and (b * s) % POST_ROWS == 0 and d % (PAIR * ATT_GROUP) == 0

    w = a_w_qkv[0]
    q, k, vT = _project(x, mix_pre_g[0], mix_pre_g[0],
                        w[:, :d].T.astype(BF16), w[:, d:2 * d].astype(BF16),
                        w[:, 2 * d:].T.astype(BF16), rope_tables=_rope_tables(s))
    lam4 = jnp.stack([a_lambda_q1[0], a_lambda_k1[0], a_lambda_q2[0], a_lambda_k2[0]])
    lambda_init = LAMBDA_INIT_BASE - LAMBDA_INIT_AMP * math.exp(-LAMBDA_INIT_RATE * 0)
    o = _attention(q, k, vT, lam4=lam4, subln_g=a_subln_g[0], lambda_init=lambda_init)
    x = _post(o.reshape(b * s, d), x.reshape(b * s, d), a_w_o[0].astype(BF16), mix_post_g[0],
              ffn_pre_g[0], ffn_w_gate_up[0].astype(BF16), ffn_w_down[0].astype(BF16),
              ffn_post_g[0]).reshape(b, s, d)

    q, k, vT = _project(x, mix_pre_g[1], kv_norm_g,
                        b_w_q[0].T.astype(BF16), kv_w[:, :d].astype(BF16),
                        kv_w[:, d:].T.astype(BF16))
    o = _attention(q, k, vT)
    x = _post(o.reshape(b * s, d), x.reshape(b * s, d), b_w_o[0].astype(BF16), mix_post_g[1],
              ffn_pre_g[1], ffn_w_gate_up[1].astype(BF16), ffn_w_down[1].astype(BF16),
              ffn_post_g[1]).reshape(b, s, d)
    return x
```

```python
import functools
import math

import jax
import jax.numpy as jnp
from jax import lax
from jax.experimental import pallas as pl
from jax.experimental.pallas import tpu as pltpu

F32 = jnp.float32
BF16 = jnp.bfloat16

EPS = 1e-6
ROPE_THETA = 500000.0
ROT_HALF = 8
HEAD_DIM = 64
PAIR = 2 * HEAD_DIM
LAMBDA_INIT_BASE, LAMBDA_INIT_AMP, LAMBDA_INIT_RATE = 0.8, 0.6, 0.3
LOG2_E = math.log2(math.e)
MASKED_LOGIT = -1e30
NEGLIGIBLE_LOG2 = -150.0

MXU_TILE = 256
ONES_ROWS = 16
ATT_BLOCK = 256
ATT_GROUP = 4
PROJ_ROWS = 512
POST_ROWS = 512
VMEM_LIMIT = 56 * 1024 * 1024

_NT = (((1,), (1,)), ((), ()))


def _rms(x):
    return x * lax.rsqrt(jnp.mean(x * x, axis=-1, keepdims=True) + EPS)


def _proj_kernel(x_ref, gq_ref, gkv_ref, wqT_ref, wk_ref, wvT_ref, *rest, rope):
    if rope:
        cosT_ref, sinT_ref, ck_ref, sk1_ref, sk2_ref, qT_ref, k_ref, vT_ref = rest
    else:
        qT_ref, k_ref, vT_ref = rest
    tm = x_ref.shape[1]
    d = x_ref.shape[2]
    n_pairs = d // PAIR
    xn = _rms(x_ref[0])
    hq = (xn * gq_ref[...]).astype(BF16)
    hkv = (xn * gkv_ref[...]).astype(BF16)

    qT = lax.dot_general(wqT_ref[...], hq, _NT, preferred_element_type=F32)
    scale = HEAD_DIM ** -0.5 * LOG2_E
    for j in range(n_pairs):
        blk = qT[j * PAIR:(j + 1) * PAIR]
        if rope:
            cos, sin = cosT_ref[...], sinT_ref[...]
            parts = []
            for c in range(2):
                b = c * HEAD_DIM
                t1, t2 = blk[b:b + ROT_HALF], blk[b + ROT_HALF:b + 2 * ROT_HALF]
                parts += [t1 * cos - t2 * sin, t2 * cos + t1 * sin,
                          blk[b + 2 * ROT_HALF:b + HEAD_DIM]]
            blk = jnp.concatenate(parts, axis=0)
        blk = (blk * scale).astype(BF16)
        for t in range(tm // ATT_BLOCK):
            qT_ref[0, j, t] = blk[:, t * ATT_BLOCK:(t + 1) * ATT_BLOCK]

    kf = jnp.dot(hkv, wk_ref[...], preferred_element_type=F32)
    for j in range(n_pairs):
        kb = kf[:, j * PAIR:(j + 1) * PAIR]
        if rope:
            kb = (kb * ck_ref[...] + pltpu.roll(kb, PAIR - ROT_HALF, 1) * sk1_ref[...]
                  + pltpu.roll(kb, ROT_HALF, 1) * sk2_ref[...])
        k_ref[0, :, j * PAIR:(j + 1) * PAIR] = kb.astype(BF16)

    vT = lax.dot_general(wvT_ref[...], hkv, _NT, preferred_element_type=F32).astype(BF16)
    for j in range(n_pairs):
        for t in range(tm // ATT_BLOCK):
            vT_ref[0, j, t] = vT[j * PAIR:(j + 1) * PAIR, t * ATT_BLOCK:(t + 1) * ATT_BLOCK]


def _project(x, gq, gkv, wqT, wk, wvT, rope_tables=None):
    b, s, d = x.shape
    tm = PROJ_ROWS
    n_pairs = d // PAIR
    nblk = s // ATT_BLOCK
    per_step = tm // ATT_BLOCK
    const = lambda bi, i: (0, 0)
    in_specs = [
        pl.BlockSpec((1, tm, d), lambda bi, i: (bi, i, 0)),
        pl.BlockSpec((1, d), const),
        pl.BlockSpec((1, d), const),
        pl.BlockSpec((d, d), const),
        pl.BlockSpec((d, d), const),
        pl.BlockSpec((d, d), const),
    ]
    args = [x, gq.reshape(1, d), gkv.reshape(1, d), wqT, wk, wvT]
    if rope_tables is not None:
        cosT, sinT, ck, sk1, sk2 = rope_tables
        in_specs += [
            pl.BlockSpec((ROT_HALF, tm), lambda bi, i: (0, i)),
            pl.BlockSpec((ROT_HALF, tm), lambda bi, i: (0, i)),
            pl.BlockSpec((tm, PAIR), lambda bi, i: (i, 0)),
            pl.BlockSpec((tm, PAIR), lambda bi, i: (i, 0)),
            pl.BlockSpec((tm, PAIR), lambda bi, i: (i, 0)),
        ]
        args += [cosT, sinT, ck, sk1, sk2]
    t_spec = pl.BlockSpec((1, n_pairs, per_step, PAIR, ATT_BLOCK), lambda bi, i: (bi, 0, i, 0, 0))
    t_shape = jax.ShapeDtypeStruct((b, n_pairs, nblk, PAIR, ATT_BLOCK), BF16)
    return pl.pallas_call(
        functools.partial(_proj_kernel, rope=rope_tables is not None),
        grid=(b, s // tm),
        in_specs=in_specs,
        out_specs=[t_spec, pl.BlockSpec((1, tm, d), lambda bi, i: (bi, i, 0)), t_spec],
        out_shape=[t_shape, jax.ShapeDtypeStruct((b, s, d), BF16), t_shape],
        compiler_params=pltpu.CompilerParams(
            dimension_semantics=("arbitrary", "arbitrary"), vmem_limit_bytes=VMEM_LIMIT),
        name="proj_rope" if rope_tables is not None else "proj",
    )(*args)


def _run_step(calls):
    running = [stage(g, ki, edge) for stage, g, ki, edge in calls]
    for gen in running:
        next(gen)
    for gen in running:
        for _ in gen:
            raise AssertionError("a stage yields exactly once")


def _pipeline_fill(stages, group, ki, causal_edge=True):
    assert len(stages) - 1 <= group
    for t in range(group):
        _run_step([(stage, t - k, ki, causal_edge)
                   for k, stage in enumerate(stages) if t - k >= 0])


def _pipeline_steady(stages, group, ki, ki_prev):
    for t in range(group):
        _run_step([(stage, (t - k) % group, ki if t >= k else ki_prev, False)
                   for k, stage in enumerate(stages)])


def _pipeline_drain(stages, group, ki):
    for t in range(len(stages) - 1):
        _run_step([(stage, group + t - k, ki, False) for k, stage in enumerate(stages) if t < k])


def _load_qpad(qT_ref, g, qi, qpad_sc):
    qT = qT_ref[0, g, qi].astype(F32)
    row = lax.broadcasted_iota(jnp.int32, qT.shape, 0)
    first = row < HEAD_DIM
    qpad_sc[g, :, :ATT_BLOCK] = jnp.where(first, qT, 0.0).astype(BF16)
    qpad_sc[g, :, ATT_BLOCK:] = jnp.where(first, 0.0, qT).astype(BF16)


def _block_positions():
    krow = lax.broadcasted_iota(jnp.int32, (ATT_BLOCK, 2 * ATT_BLOCK), 0)
    qcol = lax.broadcasted_iota(jnp.int32, (ATT_BLOCK, 2 * ATT_BLOCK), 1)
    qcol = jnp.where(qcol >= ATT_BLOCK, qcol - ATT_BLOCK, qcol)
    return krow, qcol


def _key_block(k_ref, g, ki):
    start = pl.multiple_of(ki * ATT_BLOCK, ATT_BLOCK)
    return k_ref[0, pl.ds(start, ATT_BLOCK), g * PAIR:(g + 1) * PAIR]


def _for_each_query_block(n_q, group, stages, begin, finish, older_keys_matter=None):
    def q_block(qi, carry):
        begin(qi)
        _pipeline_fill(stages, group, qi)

        if older_keys_matter is None:
            def pair(j, c):
                ki = qi - 1 - 2 * j
                _pipeline_steady(stages, group, ki, ki + 1)
                _pipeline_steady(stages, group, ki - 1, ki)
                return c

            def single(j, c):
                _pipeline_steady(stages, group, 0, 1)
                return c
            lax.fori_loop(0, qi // 2, pair, 0)
            lax.fori_loop(0, qi % 2, single, 0)
            last = 0
        else:
            _pipeline_drain(stages, group, qi)

            def whole_block(carry):
                j, _ = carry
                ki = qi - 1 - j
                _pipeline_fill(stages, group, ki, False)
                _pipeline_drain(stages, group, ki)
                return j + 1, older_keys_matter()
            lax.while_loop(lambda c: jnp.logical_and(c[0] < qi, c[1]), whole_block,
                           (jnp.int32(0), older_keys_matter()))
            finish(pl.ds(pl.multiple_of(qi * ATT_BLOCK, ATT_BLOCK), ATT_BLOCK))
            return carry
        _pipeline_drain(stages, group, last)
        finish(pl.ds(pl.multiple_of(qi * ATT_BLOCK, ATT_BLOCK), ATT_BLOCK))
        return carry
    lax.fori_loop(0, n_q, q_block, 0)


def _diff_attn_kernel(lam_ref, g_ref, qT_ref, k_ref, vT_ref, o_ref,
                      qpad_sc, s_sc, p_sc, alpha_sc, m_sc, acc_sc, *, lambda_init):
    blk = ATT_BLOCK
    group, n_q = qT_ref.shape[1], qT_ref.shape[2]
    lam4 = lam_ref[...]
    lam = (jnp.exp(jnp.sum(lam4[0:1] * lam4[1:2], axis=1, keepdims=True))
           - jnp.exp(jnp.sum(lam4[2:3] * lam4[3:4], axis=1, keepdims=True)) + lambda_init)

    def scores(g, ki, causal_edge):
        s = jnp.dot(_key_block(k_ref, g, ki), qpad_sc[g], preferred_element_type=F32)
        if causal_edge:
            krow, qcol = _block_positions()
            s = jnp.where(krow <= qcol, s, -jnp.inf)
        s_sc[g] = s
        m_prev = m_sc[g]
        m_new = jnp.maximum(m_prev, jnp.max(s, axis=0, keepdims=True))
        alpha_sc[g] = jnp.exp2(m_prev - m_new)
        m_sc[g] = m_new
        yield

    def exponentials(g, ki, causal_edge):
        p_sc[g] = jnp.exp2(s_sc[g] - m_sc[g]).astype(BF16)
        yield

    def accumulate(g, ki, causal_edge):
        v_ext = jnp.concatenate([vT_ref[0, g, ki], jnp.ones((ONES_ROWS, blk), BF16)], axis=0)
        acc_sc[g] = alpha_sc[g] * acc_sc[g] + jnp.dot(v_ext, p_sc[g], preferred_element_type=F32)
        yield

    def begin(qi):
        for g in range(group):
            _load_qpad(qT_ref, g, qi, qpad_sc)
        m_sc[...] = jnp.full(m_sc.shape, -jnp.inf, F32)
        acc_sc[...] = jnp.zeros(acc_sc.shape, F32)

    def finish(rows):
        for g in range(group):
            acc, l = acc_sc[g, :PAIR], acc_sc[g, PAIR:PAIR + 1]
            o = acc[:, :blk] / l[:, :blk] - lam * (acc[:, blk:] / l[:, blk:])
            o = o * lax.rsqrt(jnp.mean(o * o, axis=0, keepdims=True) + EPS)
            o = o * g_ref[...] * (1.0 - lambda_init)
            o_ref[0, rows, g * PAIR:(g + 1) * PAIR] = o.T.astype(BF16)

    _for_each_query_block(n_q, group, [scores, exponentials, accumulate], begin, finish)


def _stick_attn_kernel(qT_ref, k_ref, vT_ref, o_ref, qpad_sc, u_sc, z_sc, lb_sc, h_sc, row0_sc,
                       btw_sc, a_sc, c_sc, acc_sc):
    blk = ATT_BLOCK
    group, n_q = qT_ref.shape[1], qT_ref.shape[2]
    r = lax.broadcasted_iota(jnp.int32, (blk, blk), 0)
    c = lax.broadcasted_iota(jnp.int32, (blk, blk), 1)
    u_sc[...] = jnp.where(c > r, 1.0, 0.0).astype(BF16)

    def scores(g, ki, causal_edge):
        z = jnp.dot(_key_block(k_ref, g, ki), qpad_sc[g], preferred_element_type=F32)
        if causal_edge:
            krow, qcol = _block_positions()
            z = jnp.where(krow < qcol, z, MASKED_LOGIT)
        z_sc[g] = z
        yield

    def log_sigmoids(g, ki, causal_edge):
        z = z_sc[g]
        log_denom = jnp.log(1.0 + jnp.exp2(-jnp.abs(z))) * LOG2_E
        log_beta = jnp.minimum(z, 0.0) - log_denom
        log_1m = log_beta - z
        lb_sc[g] = log_beta
        h_sc[g] = log_1m.astype(BF16)
        row0_sc[g] = log_1m[0:1]
        yield

    def suffix_sums(g, ki, causal_edge):
        btw_sc[g] = jnp.dot(u_sc[...], h_sc[g], preferred_element_type=F32)
        yield

    def weights(g, ki, causal_edge):
        between, c_prev = btw_sc[g], c_sc[g]
        a_sc[g] = jnp.exp2(lb_sc[g] + between + c_prev).astype(BF16)
        c_sc[g] = c_prev + between[0:1] + row0_sc[g]
        yield

    def accumulate(g, ki, causal_edge):
        vT = vT_ref[0, g, ki]
        a = a_sc[g]
        acc_sc[g, :HEAD_DIM] += jnp.dot(vT[:HEAD_DIM], a[:, :blk], preferred_element_type=F32)
        acc_sc[g, HEAD_DIM:] += jnp.dot(vT[HEAD_DIM:], a[:, blk:], preferred_element_type=F32)
        yield

    def begin(qi):
        for g in range(group):
            _load_qpad(qT_ref, g, qi, qpad_sc)
        c_sc[...] = jnp.zeros(c_sc.shape, F32)
        acc_sc[...] = jnp.zeros(acc_sc.shape, F32)

    def finish(rows):
        for g in range(group):
            o_ref[0, rows, g * PAIR:(g + 1) * PAIR] = acc_sc[g].T.astype(BF16)

    def older_keys_matter():
        return jnp.max(c_sc[...]) > NEGLIGIBLE_LOG2

    _for_each_query_block(n_q, group, [scores, log_sigmoids, suffix_sums, weights, accumulate],
                          begin, finish, older_keys_matter)


def _attention(qT, k, vT, *, lam4=None, subln_g=None, lambda_init=None):
    b, n_pairs, nblk, _, blk = qT.shape
    s = nblk * blk
    d = k.shape[-1]
    group = ATT_GROUP
    qv_spec = pl.BlockSpec((1, group, nblk, PAIR, blk), lambda bi, h: (bi, h, 0, 0, 0))
    k_spec = pl.BlockSpec((1, s, group * PAIR), lambda bi, h: (bi, 0, h))
    o_spec = pl.BlockSpec((1, s, group * PAIR), lambda bi, h: (bi, 0, h))
    common = dict(
        grid=(b, n_pairs // group),
        out_specs=o_spec,
        out_shape=jax.ShapeDtypeStruct((b, s, d), BF16),
        compiler_params=pltpu.CompilerParams(
            dimension_semantics=("arbitrary", "arbitrary"), vmem_limit_bytes=VMEM_LIMIT),
    )
    qpad =pltpu.VMEM((group, PAIR, 2 * blk), BF16)
    stat = pltpu.VMEM((group, 1, 2 * blk), F32)
    tile_f32 = pltpu.VMEM((group, blk, 2 * blk), F32)
    tile_bf16 = pltpu.VMEM((group, blk, 2 * blk), BF16)
    if lam4 is not None:
        return pl.pallas_call(
            functools.partial(_diff_attn_kernel, lambda_init=lambda_init),
            in_specs=[pl.BlockSpec(lam4.shape, lambda bi, h: (0, 0)),
                      pl.BlockSpec((PAIR, 1), lambda bi, h: (0, 0)),
                      qv_spec, k_spec, qv_spec],
            scratch_shapes=[qpad, tile_f32, tile_bf16, stat, stat,
                            pltpu.VMEM((group, PAIR + ONES_ROWS, 2 * blk), F32)],
            name="diff_attention", **common,
        )(lam4, subln_g.reshape(PAIR, 1), qT, k, vT)
    return pl.pallas_call(
        _stick_attn_kernel,
        in_specs=[qv_spec, k_spec, qv_spec],
        scratch_shapes=[qpad, pltpu.VMEM((blk, blk), BF16), tile_f32, tile_f32, tile_bf16, stat,
                        tile_f32, tile_bf16, stat, pltpu.VMEM((group, PAIR, blk), F32)],
        name="stick_attention", **common,
    )(qT, k, vT)


def _post_kernel(o_ref, x_ref, wo_ref, gpost_ref, gpre_ref, wgu_ref, wd_ref, gfpost_ref, out_ref,
                 *, ff_bounds):
    d_ff = wd_ref.shape[0]
    tm = x_ref.shape[0]
    halves = [slice(0, tm // 2), slice(tm // 2, tm)]

    def mix(rows, _):
        m = jnp.dot(o_ref[rows], wo_ref[...], preferred_element_type=F32)
        x1 = x_ref[rows] + _rms(m) * gpost_ref[...]
        return x1, (_rms(x1) * gpre_ref[...]).astype(BF16), None

    def ffn_chunk(lo, hi):
        def stage(rows, vals):
            x1, hf, f = vals
            gate = jnp.dot(hf, wgu_ref[:, lo:hi], preferred_element_type=F32)
            up = jnp.dot(hf, wgu_ref[:, d_ff + lo:d_ff + hi], preferred_element_type=F32)
            h = (gate * jax.nn.sigmoid(gate) * up).astype(BF16)
            part = jnp.dot(h, wd_ref[lo:hi, :], preferred_element_type=F32)
            return x1, hf, part if f is None else f + part
        return stage

    def store(rows, vals):
        x1, _, f = vals
        out_ref[rows] = x1 + _rms(f) * gfpost_ref[...]

    stages = [mix] + [ffn_chunk(lo, hi) for lo, hi in zip(ff_bounds, ff_bounds[1:])] + [store]
    vals = [None] * len(halves)
    for step in range(len(halves) + len(stages) - 1):
        for k, stage in enumerate(stages):
            i = step - k
            if 0 <= i < len(halves):
                vals[i] = stage(halves[i], vals[i])


def _post(o, x, wo, gpost, gpre, wgu, wd, gfpost):
    n, d = x.shape
    d_ff = wd.shape[0]
    tm = POST_ROWS
    ff_bounds = (0, -(-d_ff // (2 * MXU_TILE)) * MXU_TILE, d_ff)
    rows = lambda i: (i, 0)
    const = lambda i: (0, 0)
    resident = functools.partial(pl.BlockSpec, index_map=const, pipeline_mode=pl.Buffered(1))
    gain = pl.BlockSpec((1, d), const)
    return pl.pallas_call(
        functools.partial(_post_kernel, ff_bounds=ff_bounds),
        grid=(n // tm,),
        in_specs=[pl.BlockSpec((tm, d), rows), pl.BlockSpec((tm, d), rows),
                  resident((d, d)), gain, gain,
                  resident((d, 2 * d_ff)), resident((d_ff, d)), gain],
        out_specs=pl.BlockSpec((tm, d), rows),
        out_shape=jax.ShapeDtypeStruct((n, d), F32),
        compiler_params=pltpu.CompilerParams(
            dimension_semantics=("arbitrary",), vmem_limit_bytes=VMEM_LIMIT),
        name="post_swiglu",
    )(o, x, wo, gpost.reshape(1, d), gpre.reshape(1, d), wgu, wd, gfpost.reshape(1, d))


def _rope_tables(seq):
    pos = jnp.arange(seq, dtype=F32)
    inv_freq = ROPE_THETA ** (-jnp.arange(0, 2 * ROT_HALF, 2, dtype=F32) / (2 * ROT_HALF))
    ang = pos[:, None] * inv_freq[None, :]
    cos, sin = jnp.cos(ang), jnp.sin(ang)
    ones = jnp.ones((seq, HEAD_DIM - 2 * ROT_HALF), F32)
    zeros = jnp.zeros((seq, HEAD_DIM - 2 * ROT_HALF), F32)
    z8 = jnp.zeros_like(sin)
    ck = jnp.tile(jnp.concatenate([cos, cos, ones], axis=1), (1, 2))
    sk1 = jnp.tile(jnp.concatenate([-sin, z8, zeros], axis=1), (1, 2))
    sk2 = jnp.tile(jnp.concatenate([z8, sin, zeros], axis=1), (1, 2))
    return cos.T, sin.T, ck, sk1, sk2


def kernel(x, a_w_qkv, a_w_o, a_lambda_q1, a_lambda_k1, a_lambda_q2, a_lambda_k2, a_subln_g,
           kv_norm_g, kv_w, b_w_q, b_w_o, mix_pre_g, mix_post_g, ffn_pre_g, ffn_post_g,
           ffn_w_gate_up, ffn_w_down):
    b, s, d = x.shape
    assert a_w_qkv.shape[0] == 1 and b_w_q.shape[0] == 1 and mix_pre_g.shape[0] == 2
    assert s % PROJ_ROWS == 0 and (b * s) % POST_ROWS == 0 and d % (PAIR * ATT_GROUP) == 0

    w = a_w_qkv[0]
    q, k, vT = _project(x, mix_pre_g[0], mix_pre_g[0],
                        w[:, :d].T.astype(BF16), w[:, d:2 * d].astype(BF16),
                        w[:, 2 * d:].T.astype(BF16), rope_tables=_rope_tables(s))
    lam4 = jnp.stack([a_lambda_q1[0], a_lambda_k1[0], a_lambda_q2[0], a_lambda_k2[0]])
    lambda_init = LAMBDA_INIT_BASE - LAMBDA_INIT_AMP * math.exp(-LAMBDA_INIT_RATE * 0)
    o = _attention(q, k, vT, lam4=lam4, subln_g=a_subln_g[0], lambda_init=lambda_init)
    x = _post(o.reshape(b * s, d), x.reshape(b * s, d), a_w_o[0].astype(BF16), mix_post_g[0],
              ffn_pre_g[0], ffn_w_gate_up[0].astype(BF16), ffn_w_down[0].astype(BF16),
              ffn_post_g[0]).reshape(b, s, d)

    q, k, vT = _project(x, mix_pre_g[1], kv_norm_g,
                        b_w_q[0].T.astype(BF16), kv_w[:, :d].astype(BF16),
                        kv_w[:, d:].T.astype(BF16))
    o = _attention(q, k, vT)
    x = _post(o.reshape(b * s, d), x.reshape(b * s, d), b_w_o[0].astype(BF16), mix_post_g[1],
              ffn_pre_g[1], ffn_w_gate_up[1].astype(BF16), ffn_w_down[1].astype(BF16),
              ffn_post_g[1]).reshape(b, s, d)
    return x
```

```python
import functools
import math

import jax
import jax.numpy as jnp
from jax import lax
from jax.experimental import pallas as pl
from jax.experimental.pallas import tpu as pltpu

F32 = jnp.float32
BF16 = jnp.bfloat16

EPS = 1e-6
ROPE_THETA = 500000.0
ROT_HALF = 8
HEAD_DIM = 64
PAIR = 2 * HEAD_DIM
LAMBDA_INIT_BASE, LAMBDA_INIT_AMP, LAMBDA_INIT_RATE = 0.8, 0.6, 0.3
LOG2_E = math.log2(math.e)
MASKED_LOGIT = -1e30
NEGLIGIBLE_LOG2 = -150.0

MXU_TILE = 256
ONES_ROWS = 16
ATT_BLOCK = 256
ATT_GROUP = 4
STEADY_UNROLLS = (4, 2, 1)
PROJ_ROWS = 512
POST_ROWS = 512
VMEM_LIMIT = 56 * 1024 * 1024

_NT = (((1,), (1,)), ((), ()))


def _rms(x):
    return x * lax.rsqrt(jnp.mean(x * x, axis=-1, keepdims=True) + EPS)


def _proj_kernel(x_ref, gq_ref, gkv_ref, wqT_ref, wk_ref, wvT_ref, *rest, rope):
    if rope:
        cosT_ref, sinT_ref, ck_ref, sk1_ref, sk2_ref, qT_ref, k_ref, vT_ref = rest
    else:
        qT_ref, k_ref, vT_ref = rest
    tm = x_ref.shape[1]
    d = x_ref.shape[2]
    n_pairs = d // PAIR
    xn = _rms(x_ref[0])
    hq = (xn * gq_ref[...]).astype(BF16)
    hkv = (xn * gkv_ref[...]).astype(BF16)

    qT = lax.dot_general(wqT_ref[...], hq, _NT, preferred_element_type=F32)
    scale = HEAD_DIM ** -0.5 * LOG2_E
    for j in range(n_pairs):
        blk = qT[j * PAIR:(j + 1) * PAIR]
        if rope:
            cos, sin = cosT_ref[...], sinT_ref[...]
            parts = []
            for c in range(2):
                b = c * HEAD_DIM
                t1, t2 = blk[b:b + ROT_HALF], blk[b + ROT_HALF:b + 2 * ROT_HALF]
                parts += [t1 * cos - t2 * sin, t2 * cos + t1 * sin,
                          blk[b + 2 * ROT_HALF:b + HEAD_DIM]]
            blk = jnp.concatenate(parts, axis=0)
        blk = (blk * scale).astype(BF16)
        for t in range(tm // ATT_BLOCK):
            qT_ref[0, j, t] = blk[:, t * ATT_BLOCK:(t + 1) * ATT_BLOCK]

    kf = jnp.dot(hkv, wk_ref[...], preferred_element_type=F32)
    for j in range(n_pairs):
        kb = kf[:, j * PAIR:(j + 1) * PAIR]
        if rope:
            kb = (kb * ck_ref[...] + pltpu.roll(kb, PAIR - ROT_HALF, 1) * sk1_ref[...]
                  + pltpu.roll(kb, ROT_HALF, 1) * sk2_ref[...])
        k_ref[0, :, j * PAIR:(j + 1) * PAIR] = kb.astype(BF16)

    vT = lax.dot_general(wvT_ref[...], hkv, _NT, preferred_element_type=F32).astype(BF16)
    for j in range(n_pairs):
        for t in range(tm // ATT_BLOCK):
            vT_ref[0, j, t] = vT[j * PAIR:(j + 1) * PAIR, t * ATT_BLOCK:(t + 1) * ATT_BLOCK]


def _project(x, gq, gkv, wqT, wk, wvT, rope_tables=None):
    b, s, d = x.shape
    tm = PROJ_ROWS
    n_pairs = d // PAIR
    nblk = s // ATT_BLOCK
    per_step = tm // ATT_BLOCK
    const = lambda bi, i: (0, 0)
    in_specs = [
        pl.BlockSpec((1, tm, d), lambda bi, i: (bi, i, 0)),
        pl.BlockSpec((1, d), const),
        pl.BlockSpec((1, d), const),
        pl.BlockSpec((d, d), const),
        pl.BlockSpec((d, d), const),
        pl.BlockSpec((d, d), const),
    ]
    args = [x, gq.reshape(1, d), gkv.reshape(1, d), wqT, wk, wvT]
    if rope_tables is not None:
        cosT, sinT, ck, sk1, sk2 = rope_tables
        in_specs += [
            pl.BlockSpec((ROT_HALF, tm), lambda bi, i: (0, i)),
            pl.BlockSpec((ROT_HALF, tm), lambda bi, i: (0, i)),
            pl.BlockSpec((tm, PAIR), lambda bi, i: (i, 0)),
            pl.BlockSpec((tm, PAIR), lambda bi, i: (i, 0)),
            pl.BlockSpec((tm, PAIR), lambda bi, i: (i, 0)),
        ]
        args += [cosT, sinT, ck, sk1, sk2]
    t_spec = pl.BlockSpec((1, n_pairs, per_step, PAIR, ATT_BLOCK), lambda bi, i: (bi, 0, i, 0, 0))
    t_shape = jax.ShapeDtypeStruct((b, n_pairs, nblk, PAIR, ATT_BLOCK), BF16)
    return pl.pallas_call(
        functools.partial(_proj_kernel, rope=rope_tables is not None),
        grid=(b, s // tm),
        in_specs=in_specs,
        out_specs=[t_spec, pl.BlockSpec((1, tm, d), lambda bi, i: (bi, i, 0)), t_spec],
        out_shape=[t_shape, jax.ShapeDtypeStruct((b, s, d), BF16), t_shape],
        compiler_params=pltpu.CompilerParams(
            dimension_semantics=("arbitrary", "arbitrary"), vmem_limit_bytes=VMEM_LIMIT),
        name="proj_rope" if rope_tables is not None else "proj",
    )(*args)


def _run_step(calls):
    running = [stage(g, ki, edge) for stage, g, ki, edge in calls]
    for gen in running:
        next(gen)
    for gen in running:
        for _ in gen:
            raise AssertionError("a stage yields exactly once")


def _pipeline_fill(stages, group, ki, causal_edge=True):
    assert len(stages) - 1 <= group
    for t in range(group):
        _run_step([(stage, t - k, ki, causal_edge)
                   for k, stage in enumerate(stages) if t - k >= 0])


def _pipeline_steady(stages, group, ki, ki_prev):
    for t in range(group):
        _run_step([(stage, (t - k) % group, ki if t >= k else ki_prev, False)
                   for k, stage in enumerate(stages)])


def _pipeline_drain(stages, group, ki):
    for t in range(len(stages) - 1):
        _run_step([(stage, group + t - k, ki, False) for k, stage in enumerate(stages) if t < k])


def _load_qpad(qT_ref, g, qi, qpad_sc):
    qT = qT_ref[0, g, qi].astype(F32)
    row = lax.broadcasted_iota(jnp.int32, qT.shape, 0)
    first = row < HEAD_DIM
    qpad_sc[g, :, :ATT_BLOCK] = jnp.where(first, qT, 0.0).astype(BF16)
    qpad_sc[g, :, ATT_BLOCK:] = jnp.where(first, 0.0, qT).astype(BF16)


def _block_positions():
    krow = lax.broadcasted_iota(jnp.int32, (ATT_BLOCK, 2 * ATT_BLOCK), 0)
    qcol = lax.broadcasted_iota(jnp.int32, (ATT_BLOCK, 2 * ATT_BLOCK), 1)
    qcol = jnp.where(qcol >= ATT_BLOCK, qcol - ATT_BLOCK, qcol)
    return krow, qcol


def _key_block(k_ref, g, ki):
    start = pl.multiple_of(ki * ATT_BLOCK, ATT_BLOCK)
    return k_ref[0, pl.ds(start, ATT_BLOCK), g * PAIR:(g + 1) * PAIR]


def _for_each_query_block(n_q, group, stages, begin, finish, older_keys_matter=None):
    def q_block(qi, carry):
        begin(qi)
        _pipeline_fill(stages, group, qi)

        if older_keys_matter is None:
            newest, left = qi - 1, qi
            for per_iter in STEADY_UNROLLS:
                def several(j, c, per_iter=per_iter, newest=newest):
                    first = newest - per_iter * j
                    for i in range(per_iter):
                        _pipeline_steady(stages, group, first - i, first - i + 1)
                    return c
                trips = left // per_iter
                lax.fori_loop(0, trips, several, 0)
                newest, left = newest - per_iter * trips, left - per_iter * trips
            last = 0
        else:
            _pipeline_drain(stages, group, qi)

            def whole_block(carry):
                j, _ = carry
                ki = qi - 1 - j
                _pipeline_fill(stages, group, ki, False)
                _pipeline_drain(stages, group, ki)
                return j + 1, older_keys_matter()
            lax.while_loop(lambda c: jnp.logical_and(c[0] < qi, c[1]), whole_block,
                           (jnp.int32(0), older_keys_matter()))
            finish(pl.ds(pl.multiple_of(qi * ATT_BLOCK, ATT_BLOCK), ATT_BLOCK))
            return carry
        _pipeline_drain(stages, group, last)
        finish(pl.ds(pl.multiple_of(qi * ATT_BLOCK, ATT_BLOCK), ATT_BLOCK))
        return carry
    lax.fori_loop(0, n_q, q_block, 0)


def _diff_attn_kernel(lam_ref, g_ref, qT_ref, k_ref, vT_ref, o_ref,
                      qpad_sc, s_sc, p_sc, alpha_sc, m_sc, acc_sc, *, lambda_init):
    blk = ATT_BLOCK
    group, n_q = qT_ref.shape[1], qT_ref.shape[2]
    lam4 = lam_ref[...]
    lam = (jnp.exp(jnp.sum(lam4[0:1] * lam4[1:2], axis=1, keepdims=True))
           - jnp.exp(jnp.sum(lam4[2:3] * lam4[3:4], axis=1, keepdims=True)) + lambda_init)

    def scores(g, ki, causal_edge):
        s = jnp.dot(_key_block(k_ref, g, ki), qpad_sc[g], preferred_element_type=F32)
        if causal_edge:
            krow, qcol = _block_positions()
            s = jnp.where(krow <= qcol, s, -jnp.inf)
        s_sc[g] = s
        m_prev = m_sc[g]
        m_new = jnp.maximum(m_prev, jnp.max(s, axis=0, keepdims=True))
        alpha_sc[g] = jnp.exp2(m_prev - m_new)
        m_sc[g] = m_new
        yield

    def exponentials(g, ki, causal_edge):
        p_sc[g] = jnp.exp2(s_sc[g] - m_sc[g]).astype(BF16)
        yield

    def accumulate(g, ki, causal_edge):
        v_ext = jnp.concatenate([vT_ref[0, g, ki], jnp.ones((ONES_ROWS, blk), BF16)], axis=0)
        acc_sc[g] = alpha_sc[g] * acc_sc[g] + jnp.dot(v_ext, p_sc[g], preferred_element_type=F32)
        yield

    def begin(qi):
        for g in range(group):
            _load_qpad(qT_ref, g, qi, qpad_sc)
        m_sc[...] = jnp.full(m_sc.shape, -jnp.inf, F32)
        acc_sc[...] = jnp.zeros(acc_sc.shape, F32)

    def finish(rows):
        for g in range(group):
            acc = acc_sc[g, :PAIR]
            inv_l = 1.0 / acc_sc[g, PAIR:PAIR + 1]
            o = acc[:, :blk] * inv_l[:, :blk] - acc[:, blk:] * (lam * inv_l[:, blk:])
            norm = lax.rsqrt(jnp.mean(o * o, axis=0, keepdims=True) + EPS)
            o = o * norm * (g_ref[...] * (1.0 - lambda_init))
            o_ref[0, rows, g * PAIR:(g + 1) * PAIR] = o.T.astype(BF16)

    _for_each_query_block(n_q, group, [scores, exponentials, accumulate], begin, finish)


def _stick_attn_kernel(qT_ref, k_ref, vT_ref, o_ref, qpad_sc, u_sc, z_sc, lb_sc, h_sc, row0_sc,
                       btw_sc, a_sc, c_sc, acc_sc):
    blk = ATT_BLOCK
    group, n_q = qT_ref.shape[1], qT_ref.shape[2]
    r = lax.broadcasted_iota(jnp.int32, (blk, blk), 0)
    c = lax.broadcasted_iota(jnp.int32, (blk, blk), 1)
    u_sc[...] = jnp.where(c > r, 1.0, 0.0).astype(BF16)

    def scores(g, ki, causal_edge):
        z = jnp.dot(_key_block(k_ref, g, ki), qpad_sc[g], preferred_element_type=F32)
        if causal_edge:
            krow, qcol = _block_positions()
            z = jnp.where(krow < qcol, z, MASKED_LOGIT)
        z_sc[g] = z
        yield

    def log_sigmoids(g, ki, causal_edge):
        z = z_sc[g]
        log_denom = jnp.log(1.0 + jnp.exp2(-jnp.abs(z))) * LOG2_E
        log_beta = jnp.minimum(z, 0.0) - log_denom
        log_1m = log_beta - z
        lb_sc[g] = log_beta
        h_sc[g] = log_1m.astype(BF16)
        row0_sc[g] = log_1m[0:1]
        yield

    def suffix_sums(g, ki, causal_edge):
        btw_sc[g] = jnp.dot(u_sc[...], h_sc[g], preferred_element_type=F32)
        yield

    def weights(g, ki, causal_edge):
        between, c_prev = btw_sc[g], c_sc[g]
        a_sc[g] = jnp.exp2(lb_sc[g] + between + c_prev).astype(BF16)
        c_sc[g] = c_prev + between[0:1] + row0_sc[g]
        yield

    def accumulate(g, ki, causal_edge):
        vT = vT_ref[0, g, ki]
        a = a_sc[g]
        acc_sc[g, :HEAD_DIM] += jnp.dot(vT[:HEAD_DIM], a[:, :blk], preferred_element_type=F32)
        acc_sc[g, HEAD_DIM:] += jnp.dot(vT[HEAD_DIM:], a[:, blk:], preferred_element_type=F32)
        yield

    def begin(qi):
        for g in range(group):
            _load_qpad(qT_ref, g, qi, qpad_sc)
        c_sc[...] = jnp.zeros(c_sc.shape, F32)
        acc_sc[...] = jnp.zeros(acc_sc.shape, F32)

    def finish(rows):
        for g in range(group):
            o_ref[0, rows, g * PAIR:(g + 1) * PAIR] = acc_sc[g].T.astype(BF16)

    def older_keys_matter():
        return jnp.max(c_sc[...]) > NEGLIGIBLE_LOG2

    _for_each_query_block(n_q, group, [scores, log_sigmoids, suffix_sums, weights, accumulate],
                          begin, finish, older_keys_matter)


def _attention(qT, k, vT, *, lam4=None, subln_g=None, lambda_init=None):
    b, n_pairs, nblk, _, blk = qT.shape
    s = nblk * blk
    d = k.shape[-1]
    diff = lam4 is not None
    group = n_pairs if diff else ATT_GROUP
    mode = dict(pipeline_mode=pl.Buffered(1)) if diff else {}
    qv_spec = pl.BlockSpec((1, group, nblk, PAIR, blk), lambda bi, h: (bi, h, 0, 0, 0), **mode)
    k_spec = pl.BlockSpec((1, s, group * PAIR), lambda bi, h: (bi, 0, h), **mode)
    o_spec = pl.BlockSpec((1, s, group * PAIR), lambda bi, h: (bi, 0, h), **mode)
    common = dict(
        grid=(b, n_pairs // group),
        out_specs=o_spec,
        out_shape=jax.ShapeDtypeStruct((b, s, d), BF16),
        compiler_params=pltpu.CompilerParams(
            dimension_semantics=("arbitrary", "arbitrary"), vmem_limit_bytes=VMEM_LIMIT),
    )
    qpad =pltpu.VMEM((group, PAIR, 2 * blk), BF16)
    stat = pltpu.VMEM((group, 1, 2 * blk), F32)
    tile_f32 = pltpu.VMEM((group, blk, 2 * blk), F32)
    tile_bf16 = pltpu.VMEM((group, blk, 2 * blk), BF16)
    if lam4 is not None:
        return pl.pallas_call(
            functools.partial(_diff_attn_kernel, lambda_init=lambda_init),
            in_specs=[pl.BlockSpec(lam4.shape, lambda bi, h: (0, 0)),
                      pl.BlockSpec((PAIR, 1), lambda bi, h: (0, 0)),
                      qv_spec, k_spec, qv_spec],
            scratch_shapes=[qpad, tile_f32, tile_bf16, stat, stat,
                            pltpu.VMEM((group, PAIR + ONES_ROWS, 2 * blk), F32)],
            name="diff_attention", **common,
        )(lam4, subln_g.reshape(PAIR, 1), qT, k, vT)
    return pl.pallas_call(
        _stick_attn_kernel,
        in_specs=[qv_spec, k_spec, qv_spec],
        scratch_shapes=[qpad, pltpu.VMEM((blk, blk), BF16), tile_f32, tile_f32, tile_bf16, stat,
                        tile_f32, tile_bf16, stat, pltpu.VMEM((group, PAIR, blk), F32)],
        name="stick_attention", **common,
    )(qT, k, vT)


def _post_kernel(o_ref, x_ref, wo_ref, gpost_ref, gpre_ref, wgu_ref, wd_ref, gfpost_ref, out_ref,
                 *, ff_bounds):
    d_ff = wd_ref.shape[0]
    tm = x_ref.shape[0]
    halves = [slice(0, tm // 2), slice(tm // 2, tm)]

    def mix(rows, _):
        m = jnp.dot(o_ref[rows], wo_ref[...], preferred_element_type=F32)
        x1 = x_ref[rows] + _rms(m) * gpost_ref[...]
        return x1, (_rms(x1) * gpre_ref[...]).astype(BF16), None

    def ffn_chunk(lo, hi):
        def stage(rows, vals):
            x1, hf, f = vals
            gate = jnp.dot(hf, wgu_ref[:, lo:hi], preferred_element_type=F32)
            up = jnp.dot(hf, wgu_ref[:, d_ff + lo:d_ff + hi], preferred_element_type=F32)
            h = (gate * jax.nn.sigmoid(gate) * up).astype(BF16)
            part = jnp.dot(h, wd_ref[lo:hi, :], preferred_element_type=F32)
            return x1, hf, part if f is None else f + part
        return stage

    def store(rows, vals):
        x1, _, f = vals
        out_ref[rows] = x1 + _rms(f) * gfpost_ref[...]

    stages = [mix] + [ffn_chunk(lo, hi) for lo, hi in zip(ff_bounds, ff_bounds[1:])] + [store]
    vals = [None] * len(halves)
    for step in range(len(halves) + len(stages) - 1):
        for k, stage in enumerate(stages):
            i = step - k
            if 0 <= i < len(halves):
                vals[i] = stage(halves[i], vals[i])


def _post(o, x, wo, gpost, gpre, wgu, wd, gfpost):
    n, d = x.shape
    d_ff = wd.shape[0]
    tm = POST_ROWS
    ff_bounds = (0, -(-d_ff // (2 * MXU_TILE)) * MXU_TILE, d_ff)
    rows = lambda i: (i, 0)
    const = lambda i: (0, 0)
    resident = functools.partial(pl.BlockSpec, index_map=const, pipeline_mode=pl.Buffered(1))
    gain = pl.BlockSpec((1, d), const)
    return pl.pallas_call(
        functools.partial(_post_kernel, ff_bounds=ff_bounds),
        grid=(n // tm,),
        in_specs=[pl.BlockSpec((tm, d), rows), pl.BlockSpec((tm, d), rows),
                  resident((d, d)), gain, gain,
                  resident((d, 2 * d_ff)), resident((d_ff, d)), gain],
        out_specs=pl.BlockSpec((tm, d), rows),
        out_shape=jax.ShapeDtypeStruct((n, d), F32),
        compiler_params=pltpu.CompilerParams(
            dimension_semantics=("arbitrary",), vmem_limit_bytes=VMEM_LIMIT),
        name="post_swiglu",
    )(o, x, wo, gpost.reshape(1, d), gpre.reshape(1, d), wgu, wd, gfpost.reshape(1, d))


def _rope_tables(seq):
    pos = jnp.arange(seq, dtype=F32)
    inv_freq = ROPE_THETA ** (-jnp.arange(0, 2 * ROT_HALF, 2, dtype=F32) / (2 * ROT_HALF))
    ang = pos[:, None] * inv_freq[None, :]
    cos, sin = jnp.cos(ang), jnp.sin(ang)
    ones = jnp.ones((seq, HEAD_DIM - 2 * ROT_HALF), F32)
    zeros = jnp.zeros((seq, HEAD_DIM - 2 * ROT_HALF), F32)
    z8 = jnp.zeros_like(sin)
    ck = jnp.tile(jnp.concatenate([cos, cos, ones], axis=1), (1, 2))
    sk1 = jnp.tile(jnp.concatenate([-sin, z8, zeros], axis=1), (1, 2))
    sk2 = jnp.tile(jnp.concatenate([z8, sin, zeros], axis=1), (1, 2))
    return cos.T, sin.T, ck, sk1, sk2


def kernel(x, a_w_qkv, a_w_o, a_lambda_q1, a_lambda_k1, a_lambda_q2, a_lambda_k2, a_subln_g,
           kv_norm_g, kv_w, b_w_q, b_w_o, mix_pre_g, mix_post_g, ffn_pre_g, ffn_post_g,
           ffn_w_gate_up, ffn_w_down):
    b, s, d = x.shape
    assert a_w_qkv.shape[0] == 1 and b_w_q.shape[0] == 1 and mix_pre_g.shape[0] == 2
    assert s % PROJ_ROWS == 0 and (b * s) % POST_ROWS == 0 and d % (PAIR * ATT_GROUP) == 0

    w = a_w_qkv[0]
    q, k, vT = _project(x, mix_pre_g[0], mix_pre_g[0],
                        w[:, :d].T.astype(BF16), w[:, d:2 * d].astype(BF16),
                        w[:, 2 * d:].T.astype(BF16), rope_tables=_rope_tables(s))
    lam4 = jnp.stack([a_lambda_q1[0], a_lambda_k1[0], a_lambda_q2[0], a_lambda_k2[0]])
    lambda_init = LAMBDA_INIT_BASE - LAMBDA_INIT_AMP * math.exp(-LAMBDA_INIT_RATE * 0)
    o = _attention(q, k, vT, lam4=lam4, subln_g=a_subln_g[0], lambda_init=lambda_init)
    x = _post(o.reshape(b * s, d), x.reshape(b * s, d), a_w_o[0].astype(BF16), mix_post_g[0],
              ffn_pre_g[0], ffn_w_gate_up[0].astype(BF16), ffn_w_down[0].astype(BF16),
              ffn_post_g[0]).reshape(b, s, d)

    q, k, vT = _project(x, mix_pre_g[1], kv_norm_g,
                        b_w_q[0].T.astype(BF16), kv_w[:, :d].astype(BF16),
                        kv_w[:, d:].T.astype(BF16))
    o = _attention(q, k, vT)
    x = _post(o.reshape(b * s, d), x.reshape(b * s, d), b_w_o[0].astype(BF16), mix_post_g[1],
              ffn_pre_g[1], ffn_w_gate_up[1].astype(BF16), ffn_w_down[1].astype(BF16),
              ffn_post_g[1]).reshape(b, s, d)
    return x
```

```python
import functools
import math

import jax
import jax.numpy as jnp
from jax import lax
from jax.experimental import pallas as pl
from jax.experimental.pallas import tpu as pltpu

F32 = jnp.float32
BF16 = jnp.bfloat16

EPS = 1e-6
ROPE_THETA = 500000.0
ROT_HALF = 8
HEAD_DIM = 64
PAIR = 2 * HEAD_DIM
LAMBDA_INIT_BASE, LAMBDA_INIT_AMP, LAMBDA_INIT_RATE = 0.8, 0.6, 0.3
LOG2_E = math.log2(math.e)
MASKED_LOGIT = -1e30
NEGLIGIBLE_LOG2 = -150.0

MXU_TILE = 256
BF16_ROWS = 16
ATT_BLOCK = 256
ATT_GROUP = 4
STEADY_UNROLLS = (4, 2, 1)
PROJ_ROWS = 512
POST_ROWS = 1024
POST_CHAIN_ROWS = 256
VMEM_LIMIT = 56 * 1024 * 1024

_NT = (((1,), (1,)), ((), ()))


def _rms(x):
    return x * lax.rsqrt(jnp.mean(x * x, axis=-1, keepdims=True) + EPS)


def _proj_kernel(x_ref, gq_ref, gkv_ref, wqT_ref, wk_ref, wvT_ref, *rest, rope):
    if rope:
        cosT_ref, sinT_ref, ck_ref, sk1_ref, sk2_ref, qT_ref, k_ref, vT_ref = rest
    else:
        qT_ref, k_ref, vT_ref = rest
    tm = x_ref.shape[1]
    d = x_ref.shape[2]
    n_pairs = d // PAIR
    xn = _rms(x_ref[0])
    hq = (xn * gq_ref[...]).astype(BF16)
    hkv = (xn * gkv_ref[...]).astype(BF16)

    qT = lax.dot_general(wqT_ref[...], hq, _NT, preferred_element_type=F32)
    scale = HEAD_DIM ** -0.5 * LOG2_E
    for j in range(n_pairs):
        blk = qT[j * PAIR:(j + 1) * PAIR]
        if rope:
            cos, sin = cosT_ref[...], sinT_ref[...]
            parts = []
            for c in range(2):
                b = c * HEAD_DIM
                t1, t2 = blk[b:b + ROT_HALF], blk[b + ROT_HALF:b + 2 * ROT_HALF]
                parts += [t1 * cos - t2 * sin, t2 * cos + t1 * sin,
                          blk[b + 2 * ROT_HALF:b + HEAD_DIM]]
            blk = jnp.concatenate(parts, axis=0)
        blk = (blk * scale).astype(BF16)
        for t in range(tm // ATT_BLOCK):
            qT_ref[0, j, t] = blk[:, t * ATT_BLOCK:(t + 1) * ATT_BLOCK]

    kf = jnp.dot(hkv, wk_ref[...], preferred_element_type=F32)
    for j in range(n_pairs):
        kb = kf[:, j * PAIR:(j + 1) * PAIR]
        if rope:
            kb = (kb * ck_ref[...] + pltpu.roll(kb, PAIR - ROT_HALF, 1) * sk1_ref[...]
                  + pltpu.roll(kb, ROT_HALF, 1) * sk2_ref[...])
        k_ref[0, :, j * PAIR:(j + 1) * PAIR] = kb.astype(BF16)

    vT = lax.dot_general(wvT_ref[...], hkv, _NT, preferred_element_type=F32).astype(BF16)
    for j in range(n_pairs):
        for t in range(tm // ATT_BLOCK):
            vT_ref[0, j, t] = vT[j * PAIR:(j + 1) * PAIR, t * ATT_BLOCK:(t + 1) * ATT_BLOCK]


def _project(x, gq, gkv, wqT, wk, wvT, rope_tables=None):
    b, s, d = x.shape
    tm = PROJ_ROWS
    n_pairs = d // PAIR
    nblk = s // ATT_BLOCK
    per_step = tm // ATT_BLOCK
    const = lambda bi, i: (0, 0)
    in_specs = [
        pl.BlockSpec((1, tm, d), lambda bi, i: (bi, i, 0)),
        pl.BlockSpec((1, d), const),
        pl.BlockSpec((1, d), const),
        pl.BlockSpec((d, d), const),
        pl.BlockSpec((d, d), const),
        pl.BlockSpec((d, d), const),
    ]
    args = [x, gq.reshape(1, d), gkv.reshape(1, d), wqT, wk, wvT]
    if rope_tables is not None:
        cosT, sinT, ck, sk1, sk2 = rope_tables
        in_specs += [
            pl.BlockSpec((ROT_HALF, tm), lambda bi, i: (0, i)),
            pl.BlockSpec((ROT_HALF, tm), lambda bi, i: (0, i)),
            pl.BlockSpec((tm, PAIR), lambda bi, i: (i, 0)),
            pl.BlockSpec((tm, PAIR), lambda bi, i: (i, 0)),
            pl.BlockSpec((tm, PAIR), lambda bi, i: (i, 0)),
        ]
        args += [cosT, sinT, ck, sk1, sk2]
    t_spec = pl.BlockSpec((1, n_pairs, per_step, PAIR, ATT_BLOCK), lambda bi, i: (bi, 0, i, 0, 0))
    t_shape = jax.ShapeDtypeStruct((b, n_pairs, nblk, PAIR, ATT_BLOCK), BF16)
    return pl.pallas_call(
        functools.partial(_proj_kernel, rope=rope_tables is not None),
        grid=(b, s // tm),
        in_specs=in_specs,
        out_specs=[t_spec, pl.BlockSpec((1, tm, d), lambda bi, i: (bi, i, 0)), t_spec],
        out_shape=[t_shape, jax.ShapeDtypeStruct((b, s, d), BF16), t_shape],
        compiler_params=pltpu.CompilerParams(
            dimension_semantics=("arbitrary", "arbitrary"), vmem_limit_bytes=VMEM_LIMIT),
        name="proj_rope" if rope_tables is not None else "proj",
    )(*args)


def _run_step(calls):
    running = [stage(g, ki, edge) for stage, g, ki, edge in calls]
    for gen in running:
        next(gen)
    for gen in running:
        for _ in gen:
            raise AssertionError("a stage yields exactly once")


def _pipeline_fill(stages, group, ki, causal_edge=True):
    assert len(stages) - 1 <= group
    for t in range(group):
        _run_step([(stage, t - k, ki, causal_edge)
                   for k, stage in enumerate(stages) if t - k >= 0])


def _pipeline_steady(stages, group, ki, ki_prev):
    for t in range(group):
        _run_step([(stage, (t - k) % group, ki if t >= k else ki_prev, False)
                   for k, stage in enumerate(stages)])


def _pipeline_drain(stages, group, ki):
    for t in range(len(stages) - 1):
        _run_step([(stage, group + t - k, ki, False) for k, stage in enumerate(stages) if t < k])


def _load_qpad(qT_ref, g, qi, qpad_sc):
    qT = qT_ref[0, g, qi].astype(F32)
    row = lax.broadcasted_iota(jnp.int32, qT.shape, 0)
    first = row < HEAD_DIM
    qpad_sc[g, :, :ATT_BLOCK] = jnp.where(first, qT, 0.0).astype(BF16)
    qpad_sc[g, :, ATT_BLOCK:] = jnp.where(first, 0.0, qT).astype(BF16)


def _block_positions():
    krow = lax.broadcasted_iota(jnp.int32, (ATT_BLOCK, 2 * ATT_BLOCK), 0)
    qcol = lax.broadcasted_iota(jnp.int32, (ATT_BLOCK, 2 * ATT_BLOCK), 1)
    qcol = jnp.where(qcol >= ATT_BLOCK, qcol - ATT_BLOCK, qcol)
    return krow, qcol


def _key_block(k_ref, g, ki):
    start = pl.multiple_of(ki * ATT_BLOCK, ATT_BLOCK)
    return k_ref[0, pl.ds(start, ATT_BLOCK), g * PAIR:(g + 1) * PAIR]


def _for_each_query_block(n_q, group, stages, begin, finish, older_keys_matter=None):
    def q_block(qi, carry):
        begin(qi)
        _pipeline_fill(stages, group, qi)

        if older_keys_matter is None:
            newest, left = qi - 1, qi
            for per_iter in STEADY_UNROLLS:
                def several(j, c, per_iter=per_iter, newest=newest):
                    first = newest - per_iter * j
                    for i in range(per_iter):
                        _pipeline_steady(stages, group, first - i, first - i + 1)
                    return c
                trips = left // per_iter
                lax.fori_loop(0, trips, several, 0)
                newest, left = newest - per_iter * trips, left - per_iter * trips
            last = 0
        else:
            _pipeline_drain(stages, group, qi)

            def whole_block(carry):
                j, _ = carry
                ki = qi - 1 - j
                _pipeline_fill(stages, group, ki, False)
                _pipeline_drain(stages, group, ki)
                return j + 1, older_keys_matter()
            lax.while_loop(lambda c: jnp.logical_and(c[0] < qi, c[1]), whole_block,
                           (jnp.int32(0), older_keys_matter()))
            finish(pl.ds(pl.multiple_of(qi * ATT_BLOCK, ATT_BLOCK), ATT_BLOCK))
            return carry
        _pipeline_drain(stages, group, last)
        finish(pl.ds(pl.multiple_of(qi * ATT_BLOCK, ATT_BLOCK), ATT_BLOCK))
        return carry
    lax.fori_loop(0, n_q, q_block, 0)


def _diff_attn_kernel(lam_ref, g_ref, qT_ref, k_ref, vT_ref, o_ref,
                      qpad_sc, s_sc, p_sc, alpha_sc, m_sc, acc_sc, *, lambda_init):
    blk = ATT_BLOCK
    group, n_q = qT_ref.shape[1], qT_ref.shape[2]
    lam4 = lam_ref[...]
    lam = (jnp.exp(jnp.sum(lam4[0:1] * lam4[1:2], axis=1, keepdims=True))
           - jnp.exp(jnp.sum(lam4[2:3] * lam4[3:4], axis=1, keepdims=True)) + lambda_init)

    def scores(g, ki, causal_edge):
        s = jnp.dot(_key_block(k_ref, g, ki), qpad_sc[g], preferred_element_type=F32)
        if causal_edge:
            krow, qcol = _block_positions()
            s = jnp.where(krow <= qcol, s, -jnp.inf)
        s_sc[g] = s
        m_prev = m_sc[g]
        m_new = jnp.maximum(m_prev, jnp.max(s, axis=0, keepdims=True))
        alpha_sc[g] = jnp.exp2(m_prev - m_new)
        m_sc[g] = m_new
        yield

    def exponentials(g, ki, causal_edge):
        p_sc[g] = jnp.exp2(s_sc[g] - m_sc[g]).astype(BF16)
        yield

    def accumulate(g, ki, causal_edge):
        v_ext = jnp.concatenate([vT_ref[0, g, ki], jnp.ones((BF16_ROWS, blk), BF16)], axis=0)
        acc_sc[g] = alpha_sc[g] * acc_sc[g] + jnp.dot(v_ext, p_sc[g], preferred_element_type=F32)
        yield

    def begin(qi):
        for g in range(group):
            _load_qpad(qT_ref, g, qi, qpad_sc)
        m_sc[...] = jnp.full(m_sc.shape, -jnp.inf, F32)
        acc_sc[...] = jnp.zeros(acc_sc.shape, F32)

    def finish(rows):
        for g in range(group):
            acc = acc_sc[g, :PAIR]
            inv_l = 1.0 / acc_sc[g, PAIR:PAIR + 1]
            o = acc[:, :blk] * inv_l[:, :blk] - acc[:, blk:] * (lam * inv_l[:, blk:])
            norm = lax.rsqrt(jnp.mean(o * o, axis=0, keepdims=True) + EPS)
            o = o * norm * (g_ref[...] * (1.0 - lambda_init))
            o_ref[0, rows, g * PAIR:(g + 1) * PAIR] = o.T.astype(BF16)

    _for_each_query_block(n_q, group, [scores, exponentials, accumulate], begin, finish)


def _stick_attn_kernel(qT_ref, k_ref, vT_ref, o_ref, qpad_sc, u_sc, z_sc, lb_sc, h_sc, row0_sc,
                       btw_sc, a_sc, c_sc, acc_sc):
    blk = ATT_BLOCK
    group, n_q = qT_ref.shape[1], qT_ref.shape[2]
    r = lax.broadcasted_iota(jnp.int32, (blk, blk), 0)
    c = lax.broadcasted_iota(jnp.int32, (blk, blk), 1)
    u_sc[...] = jnp.where(c > r, 1.0, 0.0).astype(BF16)

    def scores(g, ki, causal_edge):
        z = jnp.dot(_key_block(k_ref, g, ki), qpad_sc[g], preferred_element_type=F32)
        if causal_edge:
            krow, qcol = _block_positions()
            z = jnp.where(krow < qcol, z, MASKED_LOGIT)
        z_sc[g] = z
        yield

    def log_sigmoids(g, ki, causal_edge):
        z = z_sc[g]
        log_denom = jnp.log(1.0 + jnp.exp2(-jnp.abs(z))) * LOG2_E
        log_beta = jnp.minimum(z, 0.0) - log_denom
        log_1m = log_beta - z
        lb_sc[g] = log_beta
        h_sc[g] = log_1m.astype(BF16)
        row0_sc[g] = log_1m[0:1]
        yield

    def suffix_sums(g, ki, causal_edge):
        btw_sc[g] = jnp.dot(u_sc[...], h_sc[g], preferred_element_type=F32)
        yield

    def weights(g, ki, causal_edge):
        between, c_prev = btw_sc[g], c_sc[g]
        a_sc[g] = jnp.exp2(lb_sc[g] + between + c_prev).astype(BF16)
        c_sc[g] = c_prev + between[0:1] + row0_sc[g]
        yield

    def accumulate(g, ki, causal_edge):
        vT = vT_ref[0, g, ki]
        a = a_sc[g]
        acc_sc[g, :HEAD_DIM] += jnp.dot(vT[:HEAD_DIM], a[:, :blk], preferred_element_type=F32)
        acc_sc[g, HEAD_DIM:] += jnp.dot(vT[HEAD_DIM:], a[:, blk:], preferred_element_type=F32)
        yield

    def begin(qi):
        for g in range(group):
            _load_qpad(qT_ref, g, qi, qpad_sc)
        c_sc[...] = jnp.zeros(c_sc.shape, F32)
        acc_sc[...] = jnp.zeros(acc_sc.shape, F32)

    def finish(rows):
        for g in range(group):
            o_ref[0, rows, g * PAIR:(g + 1) * PAIR] = acc_sc[g].T.astype(BF16)

    def older_keys_matter():
        return jnp.max(c_sc[...]) > NEGLIGIBLE_LOG2

    _for_each_query_block(n_q, group, [scores, log_sigmoids, suffix_sums, weights, accumulate],
                          begin, finish, older_keys_matter)


def _attention(qT, k, vT, *, lam4=None, subln_g=None, lambda_init=None):
    b, n_pairs, nblk, _, blk = qT.shape
    s = nblk * blk
    d = k.shape[-1]
    diff = lam4 is not None
    group = n_pairs if diff else ATT_GROUP
    mode = dict(pipeline_mode=pl.Buffered(1)) if diff else {}
    qv_spec = pl.BlockSpec((1, group, nblk, PAIR, blk), lambda bi, h: (bi, h, 0, 0, 0), **mode)
    k_spec = pl.BlockSpec((1, s, group * PAIR), lambda bi, h: (bi, 0, h), **mode)
    o_spec = pl.BlockSpec((1, s, group * PAIR), lambda bi, h: (bi, 0, h), **mode)
    common = dict(
        grid=(b, n_pairs // group),
        out_specs=o_spec,
        out_shape=jax.ShapeDtypeStruct((b, s, d), BF16),
        compiler_params=pltpu.CompilerParams(
            dimension_semantics=("arbitrary", "arbitrary"), vmem_limit_bytes=VMEM_LIMIT),
    )
    qpad =pltpu.VMEM((group, PAIR, 2 * blk), BF16)
    stat = pltpu.VMEM((group, 1, 2 * blk), F32)
    tile_f32 = pltpu.VMEM((group, blk, 2 * blk), F32)
    tile_bf16 = pltpu.VMEM((group, blk, 2 * blk), BF16)
    if lam4 is not None:
        return pl.pallas_call(
            functools.partial(_diff_attn_kernel, lambda_init=lambda_init),
            in_specs=[pl.BlockSpec(lam4.shape, lambda bi, h: (0, 0)),
                      pl.BlockSpec((PAIR, 1), lambda bi, h: (0, 0)),
                      qv_spec, k_spec, qv_spec],
            scratch_shapes=[qpad, tile_f32, tile_bf16, stat, stat,
                            pltpu.VMEM((group, PAIR + BF16_ROWS, 2 * blk), F32)],
            name="diff_attention", **common,
        )(lam4, subln_g.reshape(PAIR, 1), qT, k, vT)
    return pl.pallas_call(
        _stick_attn_kernel,
        in_specs=[qv_spec, k_spec, qv_spec],
        scratch_shapes=[qpad, pltpu.VMEM((blk, blk), BF16), tile_f32, tile_f32, tile_bf16, stat,
                        tile_f32, tile_bf16, stat, pltpu.VMEM((group, PAIR, blk), F32)],
        name="stick_attention", **common,
    )(qT, k, vT)


def _post_kernel(o_ref, x_ref, wo_ref, gpost_ref, gpre_ref, wgu_ref, wd_ref, gfpost_ref, out_ref,
                 *, ff_bounds):
    d_ff = wd_ref.shape[0]
    tm = x_ref.shape[0]
    halves = [slice(r, r + POST_CHAIN_ROWS) for r in range(0, tm, POST_CHAIN_ROWS)]

    def mix(rows, _):
        m = jnp.dot(o_ref[rows], wo_ref[...], preferred_element_type=F32)
        x1 = x_ref[rows] + _rms(m) * gpost_ref[...]
        return x1, (_rms(x1) * gpre_ref[...]).astype(BF16), None

    def ffn_chunk(lo, hi):
        def stage(rows, vals):
            x1, hf, f = vals
            gate = jnp.dot(hf, wgu_ref[:, lo:hi], preferred_element_type=F32)
            up = jnp.dot(hf, wgu_ref[:, d_ff + lo:d_ff + hi], preferred_element_type=F32)
            h = (gate * jax.nn.sigmoid(gate) * up).astype(BF16)
            part = jnp.dot(h, wd_ref[lo:hi, :], preferred_element_type=F32)
            return x1, hf, part if f is None else f + part
        return stage

    def store(rows, vals):
        x1, _, f = vals
        out_ref[rows] = x1 + _rms(f) * gfpost_ref[...]

    stages = [mix] + [ffn_chunk(lo, hi) for lo, hi in zip(ff_bounds, ff_bounds[1:])] + [store]
    vals = [None] * len(halves)
    for step in range(len(halves) + len(stages) - 1):
        for k, stage in enumerate(stages):
            i = step - k
            if 0 <= i < len(halves):
                vals[i] = stage(halves[i], vals[i])


def _post(o, x, wo, gpost, gpre, wgu, wd, gfpost):
    n, d = x.shape
    d_ff = wd.shape[0]
    tm = POST_ROWS
    ff_bounds = (0, -(-d_ff // (2 * MXU_TILE)) * MXU_TILE, d_ff)
    rows = lambda i: (i, 0)
    const = lambda i: (0, 0)
    resident = functools.partial(pl.BlockSpec, index_map=const, pipeline_mode=pl.Buffered(1))
    gain = pl.BlockSpec((1, d), const)
    return pl.pallas_call(
        functools.partial(_post_kernel, ff_bounds=ff_bounds),
        grid=(n // tm,),
        in_specs=[pl.BlockSpec((tm, d), rows), pl.BlockSpec((tm, d), rows),
                  resident((d, d)), gain, gain,
                  resident((d, 2 * d_ff)), resident((d_ff, d)), gain],
        out_specs=pl.BlockSpec((tm, d), rows),
        out_shape=jax.ShapeDtypeStruct((n, d), F32),
        compiler_params=pltpu.CompilerParams(
            dimension_semantics=("arbitrary",), vmem_limit_bytes=VMEM_LIMIT),
        name="post_swiglu",
    )(o, x, wo, gpost.reshape(1, d), gpre.reshape(1, d), wgu, wd, gfpost.reshape(1, d))


def _rope_tables(seq):
    pos = jnp.arange(seq, dtype=F32)
    inv_freq = ROPE_THETA ** (-jnp.arange(0, 2 * ROT_HALF, 2, dtype=F32) / (2 * ROT_HALF))
    ang = pos[:, None] * inv_freq[None, :]
    cos, sin = jnp.cos(ang), jnp.sin(ang)
    ones = jnp.ones((seq, HEAD_DIM - 2 * ROT_HALF), F32)
    zeros = jnp.zeros((seq, HEAD_DIM - 2 * ROT_HALF), F32)
    z8 = jnp.zeros_like(sin)
    ck = jnp.tile(jnp.concatenate([cos, cos, ones], axis=1), (1, 2))
    sk1 = jnp.tile(jnp.concatenate([-sin, z8, zeros], axis=1), (1, 2))
    sk2 = jnp.tile(jnp.concatenate([z8, sin, zeros], axis=1), (1, 2))
    return cos.T, sin.T, ck, sk1, sk2


def kernel(x, a_w_qkv, a_w_o, a_lambda_q1, a_lambda_k1, a_lambda_q2, a_lambda_k2, a_subln_g,
           kv_norm_g, kv_w, b_w_q, b_w_o, mix_pre_g, mix_post_g, ffn_pre_g, ffn_post_g,
           ffn_w_gate_up, ffn_w_down):
    b, s, d = x.shape
    assert a_w_qkv.shape[0] == 1 and b_w_q.shape[0] == 1 and mix_pre_g.shape[0] == 2
    assert s % PROJ_ROWS == 0 and (b * s) % POST_ROWS == 0 and d % (PAIR * ATT_GROUP) == 0

    w = a_w_qkv[0]
    q, k, vT = _project(x, mix_pre_g[0], mix_pre_g[0],
                        w[:, :d].T.astype(BF16), w[:, d:2 * d].astype(BF16),
                        w[:, 2 * d:].T.astype(BF16), rope_tables=_rope_tables(s))
    lam4 = jnp.stack([a_lambda_q1[0], a_lambda_k1[0], a_lambda_q2[0], a_lambda_k2[0]])
    lambda_init = LAMBDA_INIT_BASE - LAMBDA_INIT_AMP * math.exp(-LAMBDA_INIT_RATE * 0)
    o = _attention(q, k, vT, lam4=lam4, subln_g=a_subln_g[0], lambda_init=lambda_init)
    x = _post(o.reshape(b * s, d), x.reshape(b * s, d), a_w_o[0].astype(BF16), mix_post_g[0],
              ffn_pre_g[0], ffn_w_gate_up[0].astype(BF16), ffn_w_down[0].astype(BF16),
              ffn_post_g[0]).reshape(b, s, d)

    q, k, vT = _project(x, mix_pre_g[1], kv_norm_g,
                        b_w_q[0].T.astype(BF16), kv_w[:, :d].astype(BF16),
                        kv_w[:, d:].T.astype(BF16))
    o = _attention(q, k, vT)
    x = _post(o.reshape(b * s, d), x.reshape(b * s, d), b_w_o[0].astype(BF16), mix_post_g[1],
              ffn_pre_g[1], ffn_w_gate_up[1].astype(BF16), ffn_w_down[1].astype(BF16),
              ffn_post_g[1]).reshape(b, s, d)
    return x
```

```python
import functools
import math

import jax
import jax.numpy as jnp
from jax import lax
from jax.experimental import pallas as pl
from jax.experimental.pallas import tpu as pltpu

F32 = jnp.float32
BF16 = jnp.bfloat16

EPS = 1e-6
ROPE_THETA = 500000.0
ROT_HALF = 8
HEAD_DIM = 64
PAIR = 2 * HEAD_DIM
LAMBDA_INIT_BASE, LAMBDA_INIT_AMP, LAMBDA_INIT_RATE = 0.8, 0.6, 0.3
LOG2_E = math.log2(math.e)
MASKED_LOGIT = -1e30
NEGLIGIBLE_LOG2 = -150.0

MXU_TILE = 256
BF16_ROWS = 16
ATT_BLOCK = 256
ATT_GROUP = 4
STEADY_UNROLLS = (4, 2, 1)
PROJ_ROWS = 512
POST_ROWS = 1024
POST_CHAIN_ROWS = 256
VMEM_LIMIT = 56 * 1024 * 1024

_NT = (((1,), (1,)), ((), ()))


def _rms(x):
    return x * lax.rsqrt(jnp.mean(x * x, axis=-1, keepdims=True) + EPS)


def _proj_kernel(x_ref, gq_ref, gkv_ref, wqT_ref, wk_ref, wvT_ref, *rest, rope):
    if rope:
        cosT_ref, sinT_ref, ck_ref, sk1_ref, sk2_ref, qT_ref, k_ref, vT_ref = rest
    else:
        qT_ref, k_ref, vT_ref = rest
    tm = x_ref.shape[1]
    d = x_ref.shape[2]
    n_pairs = d // PAIR
    xn = _rms(x_ref[0])
    hq = (xn * gq_ref[...]).astype(BF16)
    hkv = (xn * gkv_ref[...]).astype(BF16)

    qT = lax.dot_general(wqT_ref[...], hq, _NT, preferred_element_type=F32)
    scale = HEAD_DIM ** -0.5 * LOG2_E
    for j in range(n_pairs):
        blk = qT[j * PAIR:(j + 1) * PAIR]
        if rope:
            cos, sin = cosT_ref[...], sinT_ref[...]
            parts = []
            for c in range(2):
                b = c * HEAD_DIM
                t1, t2 = blk[b:b + ROT_HALF], blk[b + ROT_HALF:b + 2 * ROT_HALF]
                parts += [t1 * cos - t2 * sin, t2 * cos + t1 * sin,
                          blk[b + 2 * ROT_HALF:b + HEAD_DIM]]
            blk = jnp.concatenate(parts, axis=0)
        blk = (blk * scale).astype(BF16)
        for t in range(tm // ATT_BLOCK):
            qT_ref[0, j, t] = blk[:, t * ATT_BLOCK:(t + 1) * ATT_BLOCK]

    kf = jnp.dot(hkv, wk_ref[...], preferred_element_type=F32)
    for j in range(n_pairs):
        kb = kf[:, j * PAIR:(j + 1) * PAIR]
        if rope:
            kb = (kb * ck_ref[...] + pltpu.roll(kb, PAIR - ROT_HALF, 1) * sk1_ref[...]
                  + pltpu.roll(kb, ROT_HALF, 1) * sk2_ref[...])
        k_ref[0, :, j * PAIR:(j + 1) * PAIR] = kb.astype(BF16)

    vT = lax.dot_general(wvT_ref[...], hkv, _NT, preferred_element_type=F32).astype(BF16)
    for j in range(n_pairs):
        for t in range(tm // ATT_BLOCK):
            vT_ref[0, j, t] = vT[j * PAIR:(j + 1) * PAIR, t * ATT_BLOCK:(t + 1) * ATT_BLOCK]


def _project(x, gq, gkv, wqT, wk, wvT, rope_tables=None):
    b, s, d = x.shape
    tm = PROJ_ROWS
    n_pairs = d // PAIR
    nblk = s // ATT_BLOCK
    per_step = tm // ATT_BLOCK
    const = lambda bi, i: (0, 0)
    in_specs = [
        pl.BlockSpec((1, tm, d), lambda bi, i: (bi, i, 0)),
        pl.BlockSpec((1, d), const),
        pl.BlockSpec((1, d), const),
        pl.BlockSpec((d, d), const),
        pl.BlockSpec((d, d), const),
        pl.BlockSpec((d, d), const),
    ]
    args = [x, gq.reshape(1, d), gkv.reshape(1, d), wqT, wk, wvT]
    if rope_tables is not None:
        cosT, sinT, ck, sk1, sk2 = rope_tables
        in_specs += [
            pl.BlockSpec((ROT_HALF, tm), lambda bi, i: (0, i)),
            pl.BlockSpec((ROT_HALF, tm), lambda bi, i: (0, i)),
            pl.BlockSpec((tm, PAIR), lambda bi, i: (i, 0)),
            pl.BlockSpec((tm, PAIR), lambda bi, i: (i, 0)),
            pl.BlockSpec((tm, PAIR), lambda bi, i: (i, 0)),
        ]
        args += [cosT, sinT, ck, sk1, sk2]
    t_spec = pl.BlockSpec((1, n_pairs, per_step, PAIR, ATT_BLOCK), lambda bi, i: (bi, 0, i, 0, 0))
    t_shape = jax.ShapeDtypeStruct((b, n_pairs, nblk, PAIR, ATT_BLOCK), BF16)
    return pl.pallas_call(
        functools.partial(_proj_kernel, rope=rope_tables is not None),
        grid=(b, s // tm),
        in_specs=in_specs,
        out_specs=[t_spec, pl.BlockSpec((1, tm, d), lambda bi, i: (bi, i, 0)), t_spec],
        out_shape=[t_shape, jax.ShapeDtypeStruct((b, s, d), BF16), t_shape],
        compiler_params=pltpu.CompilerParams(
            dimension_semantics=("arbitrary", "arbitrary"), vmem_limit_bytes=VMEM_LIMIT),
        name="proj_rope" if rope_tables is not None else "proj",
    )(*args)


def _pipeline_fill(stages, group, ki, causal_edge=True):
    assert len(stages) - 1 <= group
    for t in range(group):
        for k, stage in enumerate(stages):
            if t - k >= 0:
                stage(t - k, ki, causal_edge)


def _pipeline_steady(stages, group, ki, ki_prev):
    for t in range(group):
        for k, stage in enumerate(stages):
            stage((t - k) % group, ki if t >= k else ki_prev, False)


def _pipeline_drain(stages, group, ki):
    for t in range(len(stages) - 1):
        for k, stage in enumerate(stages):
            if t < k:
                stage(group + t - k, ki, False)


def _load_qpad(qT_ref, g, qi, qpad_sc):
    qT = qT_ref[0, g, qi].astype(F32)
    row = lax.broadcasted_iota(jnp.int32, qT.shape, 0)
    first = row < HEAD_DIM
    qpad_sc[g, :, :ATT_BLOCK] = jnp.where(first, qT, 0.0).astype(BF16)
    qpad_sc[g, :, ATT_BLOCK:] = jnp.where(first, 0.0, qT).astype(BF16)


def _block_positions():
    krow = lax.broadcasted_iota(jnp.int32, (ATT_BLOCK, 2 * ATT_BLOCK), 0)
    qcol = lax.broadcasted_iota(jnp.int32, (ATT_BLOCK, 2 * ATT_BLOCK), 1)
    qcol = jnp.where(qcol >= ATT_BLOCK, qcol - ATT_BLOCK, qcol)
    return krow, qcol


def _key_block(k_ref, g, ki):
    start = pl.multiple_of(ki * ATT_BLOCK, ATT_BLOCK)
    return k_ref[0, pl.ds(start, ATT_BLOCK), g * PAIR:(g + 1) * PAIR]


def _for_each_query_block(n_q, group, stages, begin, finish, older_keys_matter=None):
    def q_block(qi, carry):
        begin(qi)
        _pipeline_fill(stages, group, qi)

        if older_keys_matter is None:
            newest, left = qi - 1, qi
            for per_iter in STEADY_UNROLLS:
                def several(j, c, per_iter=per_iter, newest=newest):
                    first = newest - per_iter * j
                    for i in range(per_iter):
                        _pipeline_steady(stages, group, first - i, first - i + 1)
                    return c
                trips = left // per_iter
                lax.fori_loop(0, trips, several, 0)
                newest, left = newest - per_iter * trips, left - per_iter * trips
            last = 0
        else:
            _pipeline_drain(stages, group, qi)

            def whole_block(carry):
                j, _ = carry
                ki = qi - 1 - j
                _pipeline_fill(stages, group, ki, False)
                _pipeline_drain(stages, group, ki)
                return j + 1, older_keys_matter()
            lax.while_loop(lambda c: jnp.logical_and(c[0] < qi, c[1]), whole_block,
                           (jnp.int32(0), older_keys_matter()))
            finish(pl.ds(pl.multiple_of(qi * ATT_BLOCK, ATT_BLOCK), ATT_BLOCK))
            return carry
        _pipeline_drain(stages, group, last)
        finish(pl.ds(pl.multiple_of(qi * ATT_BLOCK, ATT_BLOCK), ATT_BLOCK))
        return carry
    lax.fori_loop(0, n_q, q_block, 0)


def _diff_attn_kernel(lam_ref, g_ref, qT_ref, k_ref, vT_ref, o_ref,
                      qpad_sc, s_sc, p_sc, alpha_sc, m_sc, acc_sc, *, lambda_init):
    blk = ATT_BLOCK
    group, n_q = qT_ref.shape[1], qT_ref.shape[2]
    lam4 = lam_ref[...]
    lam = (jnp.exp(jnp.sum(lam4[0:1] * lam4[1:2], axis=1, keepdims=True))
           - jnp.exp(jnp.sum(lam4[2:3] * lam4[3:4], axis=1, keepdims=True)) + lambda_init)

    def scores(g, ki, causal_edge):
        s = jnp.dot(_key_block(k_ref, g, ki), qpad_sc[g], preferred_element_type=F32)
        if causal_edge:
            krow, qcol = _block_positions()
            s = jnp.where(krow <= qcol, s, -jnp.inf)
        s_sc[g] = s
        m_prev = m_sc[g]
        m_new = jnp.maximum(m_prev, jnp.max(s, axis=0, keepdims=True))
        alpha_sc[g] = jnp.exp2(m_prev - m_new)
        m_sc[g] = m_new

    def exponentials(g, ki, causal_edge):
        p_sc[g] = jnp.exp2(s_sc[g] - m_sc[g]).astype(BF16)

    def accumulate(g, ki, causal_edge):
        v_ext = jnp.concatenate([vT_ref[0, g, ki], jnp.ones((BF16_ROWS, blk), BF16)], axis=0)
        acc_sc[g] = alpha_sc[g] * acc_sc[g] + jnp.dot(v_ext, p_sc[g], preferred_element_type=F32)

    def begin(qi):
        for g in range(group):
            _load_qpad(qT_ref, g, qi, qpad_sc)
        m_sc[...] = jnp.full(m_sc.shape, -jnp.inf, F32)
        acc_sc[...] = jnp.zeros(acc_sc.shape, F32)

    def finish(rows):
        for g in range(group):
            acc = acc_sc[g, :PAIR]
            inv_l = 1.0 / acc_sc[g, PAIR:PAIR + 1]
            o = acc[:, :blk] * inv_l[:, :blk] - acc[:, blk:] * (lam * inv_l[:, blk:])
            norm = lax.rsqrt(jnp.mean(o * o, axis=0, keepdims=True) + EPS)
            o = o * norm * (g_ref[...] * (1.0 - lambda_init))
            o_ref[0, rows, g * PAIR:(g + 1) * PAIR] = o.T.astype(BF16)

    _for_each_query_block(n_q, group, [scores, exponentials, accumulate], begin, finish)


def _stick_attn_kernel(qT_ref, k_ref, vT_ref, o_ref, qpad_sc, u_sc, z_sc, lb_sc, h_sc, row0_sc,
                       btw_sc, a_sc, c_sc, acc_sc):
    blk = ATT_BLOCK
    group, n_q = qT_ref.shape[1], qT_ref.shape[2]
    r = lax.broadcasted_iota(jnp.int32, (blk, blk), 0)
    c = lax.broadcasted_iota(jnp.int32, (blk, blk), 1)
    u_sc[...] = jnp.where(c > r, 1.0, 0.0).astype(BF16)

    def scores(g, ki, causal_edge):
        z = jnp.dot(_key_block(k_ref, g, ki), qpad_sc[g], preferred_element_type=F32)
        if causal_edge:
            krow, qcol = _block_positions()
            z = jnp.where(krow < qcol, z, MASKED_LOGIT)
        z_sc[g] = z

    def log_sigmoids(g, ki, causal_edge):
        z = z_sc[g]
        log_denom = jnp.log(1.0 + jnp.exp2(-jnp.abs(z))) * LOG2_E
        log_beta = jnp.minimum(z, 0.0) - log_denom
        log_1m = log_beta - z
        lb_sc[g] = log_beta
        h_sc[g] = log_1m.astype(BF16)
        row0_sc[g] = log_1m[0:1]

    def suffix_sums(g, ki, causal_edge):
        btw_sc[g] = jnp.dot(u_sc[...], h_sc[g], preferred_element_type=F32)

    def weights(g, ki, causal_edge):
        between, c_prev = btw_sc[g], c_sc[g]
        a_sc[g] = jnp.exp2(lb_sc[g] + between + c_prev).astype(BF16)
        c_sc[g] = c_prev + between[0:1] + row0_sc[g]

    def accumulate(g, ki, causal_edge):
        vT = vT_ref[0, g, ki]
        a = a_sc[g]
        acc_sc[g, :HEAD_DIM] += jnp.dot(vT[:HEAD_DIM], a[:, :blk], preferred_element_type=F32)
        acc_sc[g, HEAD_DIM:] += jnp.dot(vT[HEAD_DIM:], a[:, blk:], preferred_element_type=F32)

    def begin(qi):
        for g in range(group):
            _load_qpad(qT_ref, g, qi, qpad_sc)
        c_sc[...] = jnp.zeros(c_sc.shape, F32)
        acc_sc[...] = jnp.zeros(acc_sc.shape, F32)

    def finish(rows):
        for g in range(group):
            o_ref[0, rows, g * PAIR:(g + 1) * PAIR] = acc_sc[g].T.astype(BF16)

    def older_keys_matter():
        return jnp.max(c_sc[...]) > NEGLIGIBLE_LOG2

    _for_each_query_block(n_q, group, [scores, log_sigmoids, suffix_sums, weights, accumulate],
                          begin, finish, older_keys_matter)


def _attention(qT, k, vT, *, lam4=None, subln_g=None, lambda_init=None):
    b, n_pairs, nblk, _, blk = qT.shape
    s = nblk * blk
    d = k.shape[-1]
    diff = lam4 is not None
    group = n_pairs if diff else ATT_GROUP
    mode = dict(pipeline_mode=pl.Buffered(1)) if diff else {}
    qv_spec = pl.BlockSpec((1, group, nblk, PAIR, blk), lambda bi, h: (bi, h, 0, 0, 0), **mode)
    k_spec = pl.BlockSpec((1, s, group * PAIR), lambda bi, h: (bi, 0, h))
    o_spec = pl.BlockSpec((1, s, group * PAIR), lambda bi, h: (bi, 0, h), **mode)
    common = dict(
        grid=(b, n_pairs // group),
        out_specs=o_spec,
        out_shape=jax.ShapeDtypeStruct((b, s, d), BF16),
        compiler_params=pltpu.CompilerParams(
            dimension_semantics=("arbitrary", "arbitrary"), vmem_limit_bytes=VMEM_LIMIT),
    )
    qpad =pltpu.VMEM((group, PAIR, 2 * blk), BF16)
    stat = pltpu.VMEM((group, 1, 2 * blk), F32)
    tile_f32 = pltpu.VMEM((group, blk, 2 * blk), F32)
    tile_bf16 = pltpu.VMEM((group, blk, 2 * blk), BF16)
    if lam4 is not None:
        return pl.pallas_call(
            functools.partial(_diff_attn_kernel, lambda_init=lambda_init),
            in_specs=[pl.BlockSpec(lam4.shape, lambda bi, h: (0, 0)),
                      pl.BlockSpec((PAIR, 1), lambda bi, h: (0, 0)),
                      qv_spec, k_spec, qv_spec],
            scratch_shapes=[qpad, tile_f32, tile_bf16, stat, stat,
                            pltpu.VMEM((group, PAIR + BF16_ROWS, 2 * blk), F32)],
            name="diff_attention", **common,
        )(lam4, subln_g.reshape(PAIR, 1), qT, k, vT)
    return pl.pallas_call(
        _stick_attn_kernel,
        in_specs=[qv_spec, k_spec, qv_spec],
        scratch_shapes=[qpad, pltpu.VMEM((blk, blk), BF16), tile_f32, tile_f32, tile_bf16, stat,
                        tile_f32, tile_bf16, stat, pltpu.VMEM((group, PAIR, blk), F32)],
        name="stick_attention", **common,
    )(qT, k, vT)


def _post_kernel(o_ref, x_ref, wo_ref, gpost_ref, gpre_ref, wgu_ref, wd_ref, gfpost_ref, out_ref,
                 *, ff_bounds):
    d_ff = wd_ref.shape[0]
    tm = x_ref.shape[0]
    halves = [slice(r, r + POST_CHAIN_ROWS) for r in range(0, tm, POST_CHAIN_ROWS)]

    def mix(rows, _):
        m = jnp.dot(o_ref[rows], wo_ref[...], preferred_element_type=F32)
        x1 = x_ref[rows] + _rms(m) * gpost_ref[...]
        return x1, (_rms(x1) * gpre_ref[...]).astype(BF16), None

    def ffn_chunk(lo, hi):
        def stage(rows, vals):
            x1, hf, f = vals
            gate = jnp.dot(hf, wgu_ref[:, lo:hi], preferred_element_type=F32)
            up = jnp.dot(hf, wgu_ref[:, d_ff + lo:d_ff + hi], preferred_element_type=F32)
            h = (gate * jax.nn.sigmoid(gate) * up).astype(BF16)
            part = jnp.dot(h, wd_ref[lo:hi, :], preferred_element_type=F32)
            return x1, hf, part if f is None else f + part
        return stage

    def store(rows, vals):
        x1, _, f = vals
        out_ref[rows] = x1 + _rms(f) * gfpost_ref[...]

    stages = [mix] + [ffn_chunk(lo, hi) for lo, hi in zip(ff_bounds, ff_bounds[1:])] + [store]
    vals = [None] * len(halves)
    for step in range(len(halves) + len(stages) - 1):
        for k, stage in enumerate(stages):
            i = step - k
            if 0 <= i < len(halves):
                vals[i] = stage(halves[i], vals[i])


def _post(o, x, wo, gpost, gpre, wgu, wd, gfpost):
    n, d = x.shape
    d_ff = wd.shape[0]
    tm = POST_ROWS
    ff_bounds = (0, -(-d_ff // (2 * MXU_TILE)) * MXU_TILE, d_ff)
    rows = lambda i: (i, 0)
    const = lambda i: (0, 0)
    resident = functools.partial(pl.BlockSpec, index_map=const, pipeline_mode=pl.Buffered(1))
    gain = pl.BlockSpec((1, d), const)
    return pl.pallas_call(
        functools.partial(_post_kernel, ff_bounds=ff_bounds),
        grid=(n // tm,),
        in_specs=[pl.BlockSpec((tm, d), rows), pl.BlockSpec((tm, d), rows),
                  resident((d, d)), gain, gain,
                  resident((d, 2 * d_ff)), resident((d_ff, d)), gain],
        out_specs=pl.BlockSpec((tm, d), rows),
        out_shape=jax.ShapeDtypeStruct((n, d), F32),
        compiler_params=pltpu.CompilerParams(
            dimension_semantics=("arbitrary",), vmem_limit_bytes=VMEM_LIMIT),
        name="post_swiglu",
    )(o, x, wo, gpost.reshape(1, d), gpre.reshape(1, d), wgu, wd, gfpost.reshape(1, d))


def _rope_tables(seq):
    pos = jnp.arange(seq, dtype=F32)
    inv_freq = ROPE_THETA ** (-jnp.arange(0, 2 * ROT_HALF, 2, dtype=F32) / (2 * ROT_HALF))
    ang = pos[:, None] * inv_freq[None, :]
    cos, sin = jnp.cos(ang), jnp.sin(ang)
    ones = jnp.ones((seq, HEAD_DIM - 2 * ROT_HALF), F32)
    zeros = jnp.zeros((seq, HEAD_DIM - 2 * ROT_HALF), F32)
    z8 = jnp.zeros_like(sin)
    ck = jnp.tile(jnp.concatenate([cos, cos, ones], axis=1), (1, 2))
    sk1 = jnp.tile(jnp.concatenate([-sin, z8, zeros], axis=1), (1, 2))
    sk2 = jnp.tile(jnp.concatenate([z8, sin, zeros], axis=1), (1, 2))
    return cos.T, sin.T, ck, sk1, sk2


def kernel(x, a_w_qkv, a_w_o, a_lambda_q1, a_lambda_k1, a_lambda_q2, a_lambda_k2, a_subln_g,
           kv_norm_g, kv_w, b_w_q, b_w_o, mix_pre_g, mix_post_g, ffn_pre_g, ffn_post_g,
           ffn_w_gate_up, ffn_w_down):
    b, s, d = x.shape
    assert a_w_qkv.shape[0] == 1 and b_w_q.shape[0] == 1 and mix_pre_g.shape[0] == 2
    assert s % PROJ_ROWS == 0 and (b * s) % POST_ROWS == 0 and d % (PAIR * ATT_GROUP) == 0

    w = a_w_qkv[0]
    q, k, vT = _project(x, mix_pre_g[0], mix_pre_g[0],
                        w[:, :d].T.astype(BF16), w[:, d:2 * d].astype(BF16),
                        w[:, 2 * d:].T.astype(BF16), rope_tables=_rope_tables(s))
    lam4 = jnp.stack([a_lambda_q1[0], a_lambda_k1[0], a_lambda_q2[0], a_lambda_k2[0]])
    lambda_init = LAMBDA_INIT_BASE - LAMBDA_INIT_AMP * math.exp(-LAMBDA_INIT_RATE * 0)
    o = _attention(q, k, vT, lam4=lam4, subln_g=a_subln_g[0], lambda_init=lambda_init)
    x = _post(o.reshape(b * s, d), x.reshape(b * s, d), a_w_o[0].astype(BF16), mix_post_g[0],
              ffn_pre_g[0], ffn_w_gate_up[0].astype(BF16), ffn_w_down[0].astype(BF16),
              ffn_post_g[0]).reshape(b, s, d)

    q, k, vT = _project(x, mix_pre_g[1], kv_norm_g,
                        b_w_q[0].T.astype(BF16), kv_w[:, :d].astype(BF16),
                        kv_w[:, d:].T.astype(BF16))
    o = _attention(q, k, vT)
    x = _post(o.reshape(b * s, d), x.reshape(b * s, d), b_w_o[0].astype(BF16), mix_post_g[1],
              ffn_pre_g[1], ffn_w_gate_up[1].astype(BF16), ffn_w_down[1].astype(BF16),
              ffn_post_g[1]).reshape(b, s, d)
    return x
```

```python
import functools
import math

import jax
import jax.numpy as jnp
from jax import lax
from jax.experimental import pallas as pl
from jax.experimental.pallas import tpu as pltpu

F32 = jnp.float32
BF16 = jnp.bfloat16

EPS = 1e-6
ROPE_THETA = 500000.0
ROT_HALF = 8
HEAD_DIM = 64
PAIR = 2 * HEAD_DIM
LAMBDA_INIT_BASE, LAMBDA_INIT_AMP, LAMBDA_INIT_RATE = 0.8, 0.6, 0.3
LOG2_E = math.log2(math.e)
MASKED_LOGIT = -1e30
NEGLIGIBLE_LOG2 = -150.0

MXU_TILE = 256
BF16_ROWS = 16
ATT_BLOCK = 256
ATT_GROUP = 4
STEADY_UNROLLS = (4, 2, 1)
PROJ_ROWS = 1024
POST_ROWS = 1024
POST_CHAIN_ROWS = 256
VMEM_LIMIT = 56 * 1024 * 1024

_NT = (((1,), (1,)), ((), ()))


def _rms(x):
    return x * lax.rsqrt(jnp.mean(x * x, axis=-1, keepdims=True) + EPS)


def _proj_kernel(x_ref, gq_ref, gkv_ref, wqT_ref, wk_ref, wvT_ref, *rest, rope):
    if rope:
        cosT_ref, sinT_ref, ck_ref, sk1_ref, sk2_ref, qT_ref, k_ref, vT_ref = rest
    else:
        qT_ref, k_ref, vT_ref = rest
    tm = x_ref.shape[1]
    d = x_ref.shape[2]
    n_pairs = d // PAIR
    xn = _rms(x_ref[0])
    hq = (xn * gq_ref[...]).astype(BF16)
    hkv = (xn * gkv_ref[...]).astype(BF16)

    qT = lax.dot_general(wqT_ref[...], hq, _NT, preferred_element_type=F32)
    scale = HEAD_DIM ** -0.5 * LOG2_E
    for j in range(n_pairs):
        blk = qT[j * PAIR:(j + 1) * PAIR]
        if rope:
            cos, sin = cosT_ref[...], sinT_ref[...]
            parts = []
            for c in range(2):
                b = c * HEAD_DIM
                t1, t2 = blk[b:b + ROT_HALF], blk[b + ROT_HALF:b + 2 * ROT_HALF]
                parts += [t1 * cos - t2 * sin, t2 * cos + t1 * sin,
                          blk[b + 2 * ROT_HALF:b + HEAD_DIM]]
            blk = jnp.concatenate(parts, axis=0)
        blk = (blk * scale).astype(BF16)
        for t in range(tm // ATT_BLOCK):
            qT_ref[0, j, t] = blk[:, t * ATT_BLOCK:(t + 1) * ATT_BLOCK]

    kf = jnp.dot(hkv, wk_ref[...], preferred_element_type=F32)
    for j in range(n_pairs):
        kb = kf[:, j * PAIR:(j + 1) * PAIR]
        if rope:
            kb = (kb * ck_ref[...] + pltpu.roll(kb, PAIR - ROT_HALF, 1) * sk1_ref[...]
                  + pltpu.roll(kb, ROT_HALF, 1) * sk2_ref[...])
        k_ref[0, :, j * PAIR:(j + 1) * PAIR] = kb.astype(BF16)

    vT = lax.dot_general(wvT_ref[...], hkv, _NT, preferred_element_type=F32).astype(BF16)
    for j in range(n_pairs):
        for t in range(tm // ATT_BLOCK):
            vT_ref[0, j, t] = vT[j * PAIR:(j + 1) * PAIR, t * ATT_BLOCK:(t + 1) * ATT_BLOCK]


def _project(x, gq, gkv, wqT, wk, wvT, rope_tables=None):
    b, s, d = x.shape
    tm = PROJ_ROWS
    n_pairs = d // PAIR
    nblk = s // ATT_BLOCK
    per_step = tm // ATT_BLOCK
    const = lambda bi, i: (0, 0)
    in_specs = [
        pl.BlockSpec((1, tm, d), lambda bi, i: (bi, i, 0)),
        pl.BlockSpec((1, d), const),
        pl.BlockSpec((1, d), const),
        pl.BlockSpec((d, d), const),
        pl.BlockSpec((d, d), const),
        pl.BlockSpec((d, d), const),
    ]
    args = [x, gq.reshape(1, d), gkv.reshape(1, d), wqT, wk, wvT]
    if rope_tables is not None:
        cosT, sinT, ck, sk1, sk2 = rope_tables
        in_specs += [
            pl.BlockSpec((ROT_HALF, tm), lambda bi, i: (0, i)),
            pl.BlockSpec((ROT_HALF, tm), lambda bi, i: (0, i)),
            pl.BlockSpec((tm, PAIR), lambda bi, i: (i, 0)),
            pl.BlockSpec((tm, PAIR), lambda bi, i: (i, 0)),
            pl.BlockSpec((tm, PAIR), lambda bi, i: (i, 0)),
        ]
        args += [cosT, sinT, ck, sk1, sk2]
    t_spec = pl.BlockSpec((1, n_pairs, per_step, PAIR, ATT_BLOCK), lambda bi, i: (bi, 0, i, 0, 0))
    t_shape = jax.ShapeDtypeStruct((b, n_pairs, nblk, PAIR, ATT_BLOCK), BF16)
    return pl.pallas_call(
        functools.partial(_proj_kernel, rope=rope_tables is not None),
        grid=(b, s // tm),
        in_specs=in_specs,
        out_specs=[t_spec, pl.BlockSpec((1, tm, d), lambda bi, i: (bi, i, 0)), t_spec],
        out_shape=[t_shape, jax.ShapeDtypeStruct((b, s, d), BF16), t_shape],
        compiler_params=pltpu.CompilerParams(
            dimension_semantics=("arbitrary", "arbitrary"), vmem_limit_bytes=VMEM_LIMIT),
        name="proj_rope" if rope_tables is not None else "proj",
    )(*args)


def _pipeline_fill(stages, group, ki, causal_edge=True):
    assert len(stages) - 1 <= group
    for t in range(group):
        for k, stage in enumerate(stages):
            if t - k >= 0:
                stage(t - k, ki, causal_edge)


def _pipeline_steady(stages, group, ki, ki_prev):
    for t in range(group):
        for k, stage in enumerate(stages):
            stage((t - k) % group, ki if t >= k else ki_prev, False)


def _pipeline_drain(stages, group, ki):
    for t in range(len(stages) - 1):
        for k, stage in enumerate(stages):
            if t < k:
                stage(group + t - k, ki, False)


def _clear_qpad(qpad_sc):
    quadrant = jnp.zeros((qpad_sc.shape[0], HEAD_DIM, ATT_BLOCK), BF16)
    qpad_sc[:, HEAD_DIM:, :ATT_BLOCK] = quadrant
    qpad_sc[:, :HEAD_DIM, ATT_BLOCK:] = quadrant


def _load_qpad(qT_ref, g, qi, qpad_sc):
    qT = qT_ref[0, g, qi]
    qpad_sc[g, :HEAD_DIM, :ATT_BLOCK] = qT[:HEAD_DIM]
    qpad_sc[g, HEAD_DIM:, ATT_BLOCK:] = qT[HEAD_DIM:]


def _block_positions():
    krow = lax.broadcasted_iota(jnp.int32, (ATT_BLOCK, 2 * ATT_BLOCK), 0)
    qcol = lax.broadcasted_iota(jnp.int32, (ATT_BLOCK, 2 * ATT_BLOCK), 1)
    qcol = jnp.where(qcol >= ATT_BLOCK, qcol - ATT_BLOCK, qcol)
    return krow, qcol


def _key_block(k_ref, g, ki):
    start = pl.multiple_of(ki * ATT_BLOCK, ATT_BLOCK)
    return k_ref[0, pl.ds(start, ATT_BLOCK), g * PAIR:(g + 1) * PAIR]


def _for_each_query_block(n_q, group, stages, begin, finish, older_keys_matter=None):
    def q_block(qi, carry):
        begin(qi)
        _pipeline_fill(stages, group, qi)

        if older_keys_matter is None:
            newest, left = qi - 1, qi
            for per_iter in STEADY_UNROLLS:
                def several(j, c, per_iter=per_iter, newest=newest):
                    first = newest - per_iter * j
                    for i in range(per_iter):
                        _pipeline_steady(stages, group, first - i, first - i + 1)
                    return c
                trips = left // per_iter
                lax.fori_loop(0, trips, several, 0)
                newest, left = newest - per_iter * trips, left - per_iter * trips
            last = 0
        else:
            always = jnp.minimum(qi, 1)

            def first_older(j, c):
                _pipeline_steady(stages, group, qi - 1, qi)
                return c
            lax.fori_loop(0, always, first_older, 0)
            _pipeline_drain(stages, group, qi - always)

            def whole_block(carry):
                j, _ = carry
                ki = qi - 1 - j
                _pipeline_fill(stages, group, ki, False)
                _pipeline_drain(stages, group, ki)
                return j + 1, older_keys_matter()
            lax.while_loop(lambda c: jnp.logical_and(c[0] < qi, c[1]), whole_block,
                           (always, older_keys_matter()))
            finish(pl.ds(pl.multiple_of(qi * ATT_BLOCK, ATT_BLOCK), ATT_BLOCK))
            return carry
        _pipeline_drain(stages, group, last)
        finish(pl.ds(pl.multiple_of(qi * ATT_BLOCK, ATT_BLOCK), ATT_BLOCK))
        return carry
    lax.fori_loop(0, n_q, q_block, 0)


def _diff_attn_kernel(lam_ref, g_ref, qT_ref, k_ref, vT_ref, o_ref,
                      qpad_sc, s_sc, p_sc, alpha_sc, m_sc, acc_sc, *, lambda_init):
    blk = ATT_BLOCK
    group, n_q = qT_ref.shape[1], qT_ref.shape[2]
    lam4 = lam_ref[...]
    lam = (jnp.exp(jnp.sum(lam4[0:1] * lam4[1:2], axis=1, keepdims=True))
           - jnp.exp(jnp.sum(lam4[2:3] * lam4[3:4], axis=1, keepdims=True)) + lambda_init)
    _clear_qpad(qpad_sc)

    def scores(g, ki, causal_edge):
        s = jnp.dot(_key_block(k_ref, g, ki), qpad_sc[g], preferred_element_type=F32)
        if causal_edge:
            krow, qcol = _block_positions()
            s = jnp.where(krow <= qcol, s, -jnp.inf)
        s_sc[g] = s
        m_prev = m_sc[g]
        m_new = jnp.maximum(m_prev, jnp.max(s, axis=0, keepdims=True))
        alpha_sc[g] = jnp.exp2(m_prev - m_new)
        m_sc[g] = m_new

    def exponentials(g, ki, causal_edge):
        p_sc[g] = jnp.exp2(s_sc[g] - m_sc[g]).astype(BF16)

    def accumulate(g, ki, causal_edge):
        v_ext = jnp.concatenate([vT_ref[0, g, ki], jnp.ones((BF16_ROWS, blk), BF16)], axis=0)
        acc_sc[g] = alpha_sc[g] * acc_sc[g] + jnp.dot(v_ext, p_sc[g], preferred_element_type=F32)

    def begin(qi):
        for g in range(group):
            _load_qpad(qT_ref, g, qi, qpad_sc)
        m_sc[...] = jnp.full(m_sc.shape, -jnp.inf, F32)
        acc_sc[...] = jnp.zeros(acc_sc.shape, F32)

    def finish(rows):
        for g in range(group):
            acc = acc_sc[g, :PAIR]
            inv_l = 1.0 / acc_sc[g, PAIR:PAIR + 1]
            o = acc[:, :blk] * inv_l[:, :blk] - acc[:, blk:] * (lam * inv_l[:, blk:])
            norm = lax.rsqrt(jnp.mean(o * o, axis=0, keepdims=True) + EPS)
            o = o * norm * (g_ref[...] * (1.0 - lambda_init))
            o_ref[0, rows, g * PAIR:(g + 1) * PAIR] = o.T.astype(BF16)

    _for_each_query_block(n_q, group, [scores, exponentials, accumulate], begin, finish)


def _stick_attn_kernel(qT_ref, k_ref, vT_ref, o_ref, qpad_sc, u_sc, z_sc, lb_sc, h_sc, row0_sc,
                       btw_sc, a_sc, c_sc, acc_sc):
    blk = ATT_BLOCK
    group, n_q = qT_ref.shape[1], qT_ref.shape[2]
    r = lax.broadcasted_iota(jnp.int32, (blk, blk), 0)
    c = lax.broadcasted_iota(jnp.int32, (blk, blk), 1)
    u_sc[...] = jnp.where(c > r, 1.0, 0.0).astype(BF16)
    _clear_qpad(qpad_sc)

    def scores(g, ki, causal_edge):
        z = jnp.dot(_key_block(k_ref, g, ki), qpad_sc[g], preferred_element_type=F32)
        if causal_edge:
            krow, qcol = _block_positions()
            z = jnp.where(krow < qcol, z, MASKED_LOGIT)
        z_sc[g] = z

    def log_sigmoids(g, ki, causal_edge):
        z = z_sc[g]
        log_denom = jnp.log(1.0 + jnp.exp2(-jnp.abs(z))) * LOG2_E
        log_beta = jnp.minimum(z, 0.0) - log_denom
        log_1m = log_beta - z
        lb_sc[g] = log_beta
        h_sc[g] = log_1m.astype(BF16)
        row0_sc[g] = log_1m[0:1]

    def suffix_sums(g, ki, causal_edge):
        btw_sc[g] = jnp.dot(u_sc[...], h_sc[g], preferred_element_type=F32)

    def weights(g, ki, causal_edge):
        between, c_prev = btw_sc[g], c_sc[g]
        a_sc[g] = jnp.exp2(lb_sc[g] + between + c_prev).astype(BF16)
        c_sc[g] = c_prev + between[0:1] + row0_sc[g]

    def accumulate(g, ki, causal_edge):
        vT = vT_ref[0, g, ki]
        a = a_sc[g]
        acc_sc[g, :HEAD_DIM] += jnp.dot(vT[:HEAD_DIM], a[:, :blk], preferred_element_type=F32)
        acc_sc[g, HEAD_DIM:] += jnp.dot(vT[HEAD_DIM:], a[:, blk:], preferred_element_type=F32)

    def begin(qi):
        for g in range(group):
            _load_qpad(qT_ref, g, qi, qpad_sc)
        c_sc[...] = jnp.zeros(c_sc.shape, F32)
        acc_sc[...] = jnp.zeros(acc_sc.shape, F32)

    def finish(rows):
        for g in range(group):
            o_ref[0, rows, g * PAIR:(g + 1) * PAIR] = acc_sc[g].T.astype(BF16)

    def older_keys_matter():
        return jnp.max(c_sc[...]) > NEGLIGIBLE_LOG2

    _for_each_query_block(n_q, group, [scores, log_sigmoids, suffix_sums, weights, accumulate],
                          begin, finish, older_keys_matter)


def _attention(qT, k, vT, *, lam4=None, subln_g=None, lambda_init=None):
    b, n_pairs, nblk, _, blk = qT.shape
    s = nblk * blk
    d = k.shape[-1]
    diff = lam4 is not None
    group = n_pairs if diff else ATT_GROUP
    mode = dict(pipeline_mode=pl.Buffered(1)) if diff else {}
    qv_spec = pl.BlockSpec((1, group, nblk, PAIR, blk), lambda bi, h: (bi, h, 0, 0, 0), **mode)
    k_spec = pl.BlockSpec((1, s, group * PAIR), lambda bi, h: (bi, 0, h))
    o_spec = pl.BlockSpec((1, s, group * PAIR), lambda bi, h: (bi, 0, h), **mode)
    common = dict(
        grid=(b, n_pairs // group),
        out_specs=o_spec,
        out_shape=jax.ShapeDtypeStruct((b, s, d), BF16),
        compiler_params=pltpu.CompilerParams(
            dimension_semantics=("arbitrary", "arbitrary"), vmem_limit_bytes=VMEM_LIMIT),
    )
    qpad =pltpu.VMEM((group, PAIR, 2 * blk), BF16)
    stat = pltpu.VMEM((group, 1, 2 * blk), F32)
    tile_f32 = pltpu.VMEM((group, blk, 2 * blk), F32)
    tile_bf16 = pltpu.VMEM((group, blk, 2 * blk), BF16)
    if lam4 is not None:
        return pl.pallas_call(
            functools.partial(_diff_attn_kernel, lambda_init=lambda_init),
            in_specs=[pl.BlockSpec(lam4.shape, lambda bi, h: (0, 0)),
                      pl.BlockSpec((PAIR, 1), lambda bi, h: (0, 0)),
                      qv_spec, k_spec, qv_spec],
            scratch_shapes=[qpad, tile_f32, tile_bf16, stat, stat,
                            pltpu.VMEM((group, PAIR + BF16_ROWS, 2 * blk), F32)],
            name="diff_attention", **common,
        )(lam4, subln_g.reshape(PAIR, 1), qT, k, vT)
    return pl.pallas_call(
        _stick_attn_kernel,
        in_specs=[qv_spec, k_spec, qv_spec],
        scratch_shapes=[qpad, pltpu.VMEM((blk, blk), BF16), tile_f32, tile_f32, tile_bf16, stat,
                        tile_f32, tile_bf16, stat, pltpu.VMEM((group, PAIR, blk), F32)],
        name="stick_attention", **common,
    )(qT, k, vT)


def _post_kernel(o_ref, x_ref, wo_ref, gpost_ref, gpre_ref, wgu_ref, wd_ref, gfpost_ref, out_ref,
                 *, ff_bounds):
    d_ff = wd_ref.shape[0]
    tm = x_ref.shape[0]
    halves = [slice(r, r + POST_CHAIN_ROWS) for r in range(0, tm, POST_CHAIN_ROWS)]

    def mix(rows, _):
        m = jnp.dot(o_ref[rows], wo_ref[...], preferred_element_type=F32)
        x1 = x_ref[rows] + _rms(m) * gpost_ref[...]
        return x1, (_rms(x1) * gpre_ref[...]).astype(BF16), None

    def ffn_chunk(lo, hi):
        def stage(rows, vals):
            x1, hf, f = vals
            gate = jnp.dot(hf, wgu_ref[:, lo:hi], preferred_element_type=F32)
            up = jnp.dot(hf, wgu_ref[:, d_ff + lo:d_ff + hi], preferred_element_type=F32)
            h = (gate * jax.nn.sigmoid(gate) * up).astype(BF16)
            part = jnp.dot(h, wd_ref[lo:hi, :], preferred_element_type=F32)
            return x1, hf, part if f is None else f + part
        return stage

    def store(rows, vals):
        x1, _, f = vals
        out_ref[rows] = x1 + _rms(f) * gfpost_ref[...]

    stages = [mix] + [ffn_chunk(lo, hi) for lo, hi in zip(ff_bounds, ff_bounds[1:])] + [store]
    vals = [None] * len(halves)
    for step in range(len(halves) + len(stages) - 1):
        for k, stage in enumerate(stages):
            i = step - k
            if 0 <= i < len(halves):
                vals[i] = stage(halves[i], vals[i])


def _post(o, x, wo, gpost, gpre, wgu, wd, gfpost):
    n, d = x.shape
    d_ff = wd.shape[0]
    tm = POST_ROWS
    ff_bounds = (0, -(-d_ff // (2 * MXU_TILE)) * MXU_TILE, d_ff)
    rows = lambda i: (i, 0)
    const = lambda i: (0, 0)
    resident = functools.partial(pl.BlockSpec, index_map=const, pipeline_mode=pl.Buffered(1))
    gain = pl.BlockSpec((1, d), const)
    return pl.pallas_call(
        functools.partial(_post_kernel, ff_bounds=ff_bounds),
        grid=(n // tm,),
        in_specs=[pl.BlockSpec((tm, d), rows), pl.BlockSpec((tm, d), rows),
                  resident((d, d)), gain, gain,
                  resident((d, 2 * d_ff)), resident((d_ff, d)), gain],
        out_specs=pl.BlockSpec((tm, d), rows),
        out_shape=jax.ShapeDtypeStruct((n, d), F32),
        compiler_params=pltpu.CompilerParams(
            dimension_semantics=("arbitrary",), vmem_limit_bytes=VMEM_LIMIT),
        name="post_swiglu",
    )(o, x, wo, gpost.reshape(1, d), gpre.reshape(1, d), wgu, wd, gfpost.reshape(1, d))


def _rope_tables(seq):
    pos = jnp.arange(seq, dtype=F32)
    inv_freq = ROPE_THETA ** (-jnp.arange(0, 2 * ROT_HALF, 2, dtype=F32) / (2 * ROT_HALF))
    ang = pos[:, None] * inv_freq[None, :]
    cos, sin = jnp.cos(ang), jnp.sin(ang)
    ones = jnp.ones((seq, HEAD_DIM - 2 * ROT_HALF), F32)
    zeros = jnp.zeros((seq, HEAD_DIM - 2 * ROT_HALF), F32)
    z8 = jnp.zeros_like(sin)
    ck = jnp.tile(jnp.concatenate([cos, cos, ones], axis=1), (1, 2))
    sk1 = jnp.tile(jnp.concatenate([-sin, z8, zeros], axis=1), (1, 2))
    sk2 = jnp.tile(jnp.concatenate([z8, sin, zeros], axis=1), (1, 2))
    return cos.T, sin.T, ck, sk1, sk2


def kernel(x, a_w_qkv, a_w_o, a_lambda_q1, a_lambda_k1, a_lambda_q2, a_lambda_k2, a_subln_g,
           kv_norm_g, kv_w, b_w_q, b_w_o, mix_pre_g, mix_post_g, ffn_pre_g, ffn_post_g,
           ffn_w_gate_up, ffn_w_down):
    b, s, d = x.shape
    assert a_w_qkv.shape[0] == 1 and b_w_q.shape[0] == 1 and mix_pre_g.shape[0] == 2
    assert s % PROJ_ROWS == 0 and (b * s) % POST_ROWS == 0 and d % (PAIR * ATT_GROUP) == 0

    w = a_w_qkv[0]
    q, k, vT = _project(x, mix_pre_g[0], mix_pre_g[0],
                        w[:, :d].T.astype(BF16), w[:, d:2 * d].astype(BF16),
                        w[:, 2 * d:].T.astype(BF16), rope_tables=_rope_tables(s))
    lam4 = jnp.stack([a_lambda_q1[0], a_lambda_k1[0], a_lambda_q2[0], a_lambda_k2[0]])
    lambda_init = LAMBDA_INIT_BASE - LAMBDA_INIT_AMP * math.exp(-LAMBDA_INIT_RATE * 0)
    o = _attention(q, k, vT, lam4=lam4, subln_g=a_subln_g[0], lambda_init=lambda_init)
    x = _post(o.reshape(b * s, d), x.reshape(b * s, d), a_w_o[0].astype(BF16), mix_post_g[0],
              ffn_pre_g[0], ffn_w_gate_up[0].astype(BF16), ffn_w_down[0].astype(BF16),
              ffn_post_g[0]).reshape(b, s, d)

    q, k, vT = _project(x, mix_pre_g[1], kv_norm_g,
                        b_w_q[0].T.astype(BF16), kv_w[:, :d].astype(BF16),
                        kv_w[:, d:].T.astype(BF16))
    o = _attention(q, k, vT)
    x = _post(o.reshape(b * s, d), x.reshape(b * s, d), b_w_o[0].astype(BF16), mix_post_g[1],
              ffn_pre_g[1], ffn_w_gate_up[1].astype(BF16), ffn_w_down[1].astype(BF16),
              ffn_post_g[1]).reshape(b, s, d)
    return x
```

```python
import functools
import math

import jax
import jax.numpy as jnp
from jax import lax
from jax.experimental import pallas as pl
from jax.experimental.pallas import tpu as pltpu

F32 = jnp.float32
BF16 = jnp.bfloat16

EPS = 1e-6
ROPE_THETA = 500000.0
ROT_HALF = 8
HEAD_DIM = 64
PAIR = 2 * HEAD_DIM
LAMBDA_INIT_BASE, LAMBDA_INIT_AMP, LAMBDA_INIT_RATE = 0.8, 0.6, 0.3
LOG2_E = math.log2(math.e)
MASKED_LOGIT = -1e30
NEGLIGIBLE_LOG2 = -150.0

MXU_TILE = 256
BF16_ROWS = 16
ATT_BLOCK = 256
ATT_GROUP = 4
STEADY_UNROLLS = (4, 2, 1)
PROJ_ROWS = 1024
POST_ROWS = 1024
POST_CHAIN_ROWS = 256
VMEM_LIMIT = 56 * 1024 * 1024

_NT = (((1,), (1,)), ((), ()))


def _rms(x):
    return x * lax.rsqrt(jnp.mean(x * x, axis=-1, keepdims=True) + EPS)


def _proj_kernel(x_ref, gq_ref, gkv_ref, wqT_ref, wk_ref, wvT_ref, *rest, rope):
    if rope:
        cosT_ref, sinT_ref, ck_ref, sk1_ref, sk2_ref, qT_ref, k_ref, vT_ref = rest
    else:
        qT_ref, k_ref, vT_ref = rest
    tm = x_ref.shape[1]
    d = x_ref.shape[2]
    n_pairs = d // PAIR
    xn = _rms(x_ref[0])
    hq = (xn * gq_ref[...]).astype(BF16)
    hkv = (xn * gkv_ref[...]).astype(BF16)

    qT = lax.dot_general(wqT_ref[...], hq, _NT, preferred_element_type=F32)
    scale = HEAD_DIM ** -0.5 * LOG2_E
    for j in range(n_pairs):
        blk = qT[j * PAIR:(j + 1) * PAIR]
        if rope:
            cos, sin = cosT_ref[...], sinT_ref[...]
            parts = []
            for c in range(2):
                b = c * HEAD_DIM
                t1, t2 = blk[b:b + ROT_HALF], blk[b + ROT_HALF:b + 2 * ROT_HALF]
                parts += [t1 * cos - t2 * sin, t2 * cos + t1 * sin,
                          blk[b + 2 * ROT_HALF:b + HEAD_DIM]]
            blk = jnp.concatenate(parts, axis=0)
        blk = (blk * scale).astype(BF16)
        for t in range(tm // ATT_BLOCK):
            qT_ref[0, j, t] = blk[:, t * ATT_BLOCK:(t + 1) * ATT_BLOCK]

    kf = jnp.dot(hkv, wk_ref[...], preferred_element_type=F32)
    for j in range(n_pairs):
        kb = kf[:, j * PAIR:(j + 1) * PAIR]
        if rope:
            kb = (kb * ck_ref[...] + pltpu.roll(kb, PAIR - ROT_HALF, 1) * sk1_ref[...]
                  + pltpu.roll(kb, ROT_HALF, 1) * sk2_ref[...])
        k_ref[0, :, j * PAIR:(j + 1) * PAIR] = kb.astype(BF16)

    vT = lax.dot_general(wvT_ref[...], hkv, _NT, preferred_element_type=F32).astype(BF16)
    for j in range(n_pairs):
        for t in range(tm // ATT_BLOCK):
            vT_ref[0, j, t] = vT[j * PAIR:(j + 1) * PAIR, t * ATT_BLOCK:(t + 1) * ATT_BLOCK]


def _project(x, gq, gkv, wqT, wk, wvT, rope_tables=None):
    b, s, d = x.shape
    tm = PROJ_ROWS
    n_pairs = d // PAIR
    nblk = s // ATT_BLOCK
    per_step = tm // ATT_BLOCK
    const = lambda bi, i: (0, 0)
    in_specs = [
        pl.BlockSpec((1, tm, d), lambda bi, i: (bi, i, 0)),
        pl.BlockSpec((1, d), const),
        pl.BlockSpec((1, d), const),
        pl.BlockSpec((d, d), const),
        pl.BlockSpec((d, d), const),
        pl.BlockSpec((d, d), const),
    ]
    args = [x, gq.reshape(1, d), gkv.reshape(1, d), wqT, wk, wvT]
    if rope_tables is not None:
        cosT, sinT, ck, sk1, sk2 = rope_tables
        in_specs += [
            pl.BlockSpec((ROT_HALF, tm), lambda bi, i: (0, i)),
            pl.BlockSpec((ROT_HALF, tm), lambda bi, i: (0, i)),
            pl.BlockSpec((tm, PAIR), lambda bi, i: (i, 0)),
            pl.BlockSpec((tm, PAIR), lambda bi, i: (i, 0)),
            pl.BlockSpec((tm, PAIR), lambda bi, i: (i, 0)),
        ]
        args += [cosT, sinT, ck, sk1, sk2]
    t_spec = pl.BlockSpec((1, n_pairs, per_step, PAIR, ATT_BLOCK), lambda bi, i: (bi, 0, i, 0, 0))
    t_shape = jax.ShapeDtypeStruct((b, n_pairs, nblk, PAIR, ATT_BLOCK), BF16)
    return pl.pallas_call(
        functools.partial(_proj_kernel, rope=rope_tables is not None),
        grid=(b, s // tm),
        in_specs=in_specs,
        out_specs=[t_spec, pl.BlockSpec((1, tm, d), lambda bi, i: (bi, i, 0)), t_spec],
        out_shape=[t_shape, jax.ShapeDtypeStruct((b, s, d), BF16), t_shape],
        compiler_params=pltpu.CompilerParams(
            dimension_semantics=("arbitrary", "arbitrary"), vmem_limit_bytes=VMEM_LIMIT),
        name="proj_rope" if rope_tables is not None else "proj",
    )(*args)


def _pipeline_fill(stages, group, ki, causal_edge=True):
    assert len(stages) - 1 <= group
    for t in range(group):
        for k, stage in enumerate(stages):
            if t - k >= 0:
                stage(t - k, ki, causal_edge)


def _pipeline_steady(stages, group, ki, ki_prev):
    for t in range(group):
        for k, stage in enumerate(stages):
            stage((t - k) % group, ki if t >= k else ki_prev, False)


def _pipeline_drain(stages, group, ki):
    for t in range(len(stages) - 1):
        for k, stage in enumerate(stages):
            if t < k:
                stage(group + t - k, ki, False)


def _clear_qpad(qpad_sc):
    quadrant = jnp.zeros((qpad_sc.shape[0], HEAD_DIM, ATT_BLOCK), BF16)
    qpad_sc[:, HEAD_DIM:, :ATT_BLOCK] = quadrant
    qpad_sc[:, :HEAD_DIM, ATT_BLOCK:] = quadrant


def _load_qpad(qT_ref, g, qi, qpad_sc):
    qT = qT_ref[0, g, qi]
    qpad_sc[g, :HEAD_DIM, :ATT_BLOCK] = qT[:HEAD_DIM]
    qpad_sc[g, HEAD_DIM:, ATT_BLOCK:] = qT[HEAD_DIM:]


def _block_positions():
    krow = lax.broadcasted_iota(jnp.int32, (ATT_BLOCK, 2 * ATT_BLOCK), 0)
    qcol = lax.broadcasted_iota(jnp.int32, (ATT_BLOCK, 2 * ATT_BLOCK), 1)
    qcol = jnp.where(qcol >= ATT_BLOCK, qcol - ATT_BLOCK, qcol)
    return krow, qcol


def _key_block(k_ref, g, ki):
    start = pl.multiple_of(ki * ATT_BLOCK, ATT_BLOCK)
    return k_ref[0, pl.ds(start, ATT_BLOCK), g * PAIR:(g + 1) * PAIR]


def _for_each_query_block(n_q, group, stages, begin, finish, older_keys_matter=None):
    def q_block(qi, carry):
        begin(qi)
        _pipeline_fill(stages, group, qi)

        if older_keys_matter is None:
            newest, left = qi - 1, qi
            for per_iter in STEADY_UNROLLS:
                def several(j, c, per_iter=per_iter, newest=newest):
                    first = newest - per_iter * j
                    for i in range(per_iter):
                        _pipeline_steady(stages, group, first - i, first - i + 1)
                    return c
                trips = left // per_iter
                lax.fori_loop(0, trips, several, 0)
                newest, left = newest - per_iter * trips, left - per_iter * trips
            last = 0
        else:
            _pipeline_drain(stages, group, qi)

            def whole_block(carry):
                j, _ = carry
                ki = qi - 1 - j
                _pipeline_fill(stages, group, ki, False)
                _pipeline_drain(stages, group, ki)
                return j + 1, older_keys_matter()
            lax.while_loop(lambda c: jnp.logical_and(c[0] < qi, c[1]), whole_block,
                           (jnp.int32(0), True))
            finish(pl.ds(pl.multiple_of(qi * ATT_BLOCK, ATT_BLOCK), ATT_BLOCK))
            return carry
        _pipeline_drain(stages, group, last)
        finish(pl.ds(pl.multiple_of(qi * ATT_BLOCK, ATT_BLOCK), ATT_BLOCK))
        return carry
    lax.fori_loop(0, n_q, q_block, 0)


def _diff_attn_kernel(lam_ref, g_ref, qT_ref, k_ref, vT_ref, o_ref,
                      qpad_sc, s_sc, p_sc, alpha_sc, m_sc, acc_sc, *, lambda_init):
    blk = ATT_BLOCK
    group, n_q = qT_ref.shape[1], qT_ref.shape[2]
    lam4 = lam_ref[...]
    lam = (jnp.exp(jnp.sum(lam4[0:1] * lam4[1:2], axis=1, keepdims=True))
           - jnp.exp(jnp.sum(lam4[2:3] * lam4[3:4], axis=1, keepdims=True)) + lambda_init)
    _clear_qpad(qpad_sc)

    def scores(g, ki, causal_edge):
        s = jnp.dot(_key_block(k_ref, g, ki), qpad_sc[g], preferred_element_type=F32)
        if causal_edge:
            krow, qcol = _block_positions()
            s = jnp.where(krow <= qcol, s, -jnp.inf)
        s_sc[g] = s
        m_prev = m_sc[g]
        m_new = jnp.maximum(m_prev, jnp.max(s, axis=0, keepdims=True))
        alpha_sc[g] = jnp.exp2(m_prev - m_new)
        m_sc[g] = m_new

    def exponentials(g, ki, causal_edge):
        p_sc[g] = jnp.exp2(s_sc[g] - m_sc[g]).astype(BF16)

    def accumulate(g, ki, causal_edge):
        v_ext = jnp.concatenate([vT_ref[0, g, ki], jnp.ones((BF16_ROWS, blk), BF16)], axis=0)
        acc_sc[g] = alpha_sc[g] * acc_sc[g] + jnp.dot(v_ext, p_sc[g], preferred_element_type=F32)

    def begin(qi):
        for g in range(group):
            _load_qpad(qT_ref, g, qi, qpad_sc)
        m_sc[...] = jnp.full(m_sc.shape, -jnp.inf, F32)
        acc_sc[...] = jnp.zeros(acc_sc.shape, F32)

    def finish(rows):
        for g in range(group):
            acc = acc_sc[g, :PAIR]
            inv_l = 1.0 / acc_sc[g, PAIR:PAIR + 1]
            o = acc[:, :blk] * inv_l[:, :blk] - acc[:, blk:] * (lam * inv_l[:, blk:])
            norm = lax.rsqrt(jnp.mean(o * o, axis=0, keepdims=True) + EPS)
            o = o * norm * (g_ref[...] * (1.0 - lambda_init))
            o_ref[0, rows, g * PAIR:(g + 1) * PAIR] = o.T.astype(BF16)

    _for_each_query_block(n_q, group, [scores, exponentials, accumulate], begin, finish)


def _stick_attn_kernel(qT_ref, k_ref, vT_ref, o_ref, qpad_sc, u_sc, z_sc, lb_sc, h_sc, row0_sc,
                       btw_sc, a_sc, c_sc, acc_sc):
    blk = ATT_BLOCK
    group, n_q = qT_ref.shape[1], qT_ref.shape[2]
    r = lax.broadcasted_iota(jnp.int32, (blk, blk), 0)
    c = lax.broadcasted_iota(jnp.int32, (blk, blk), 1)
    u_sc[...] = jnp.where(c > r, 1.0, 0.0).astype(BF16)
    _clear_qpad(qpad_sc)

    def scores(g, ki, causal_edge):
        z = jnp.dot(_key_block(k_ref, g, ki), qpad_sc[g], preferred_element_type=F32)
        if causal_edge:
            krow, qcol = _block_positions()
            z = jnp.where(krow < qcol, z, MASKED_LOGIT)
        z_sc[g] = z

    def log_sigmoids(g, ki, causal_edge):
        z = z_sc[g]
        log_denom = jnp.log(1.0 + jnp.exp2(-jnp.abs(z))) * LOG2_E
        log_beta = jnp.minimum(z, 0.0) - log_denom
        log_1m = log_beta - z
        lb_sc[g] = log_beta
        h_sc[g] = log_1m.astype(BF16)
        row0_sc[g] = log_1m[0:1]

    def suffix_sums(g, ki, causal_edge):
        btw_sc[g] = jnp.dot(u_sc[...], h_sc[g], preferred_element_type=F32)

    def weights(g, ki, causal_edge):
        between, c_prev = btw_sc[g], c_sc[g]
        a_sc[g] = jnp.exp2(lb_sc[g] + between + c_prev).astype(BF16)
        c_sc[g] = c_prev + between[0:1] + row0_sc[g]

    def accumulate(g, ki, causal_edge):
        vT = vT_ref[0, g, ki]
        a = a_sc[g]
        acc_sc[g, :HEAD_DIM] += jnp.dot(vT[:HEAD_DIM], a[:, :blk], preferred_element_type=F32)
        acc_sc[g, HEAD_DIM:] += jnp.dot(vT[HEAD_DIM:], a[:, blk:], preferred_element_type=F32)

    def begin(qi):
        for g in range(group):
            _load_qpad(qT_ref, g, qi, qpad_sc)
        c_sc[...] = jnp.zeros(c_sc.shape, F32)
        acc_sc[...] = jnp.zeros(acc_sc.shape, F32)

    def finish(rows):
        for g in range(group):
            o_ref[0, rows, g * PAIR:(g + 1) * PAIR] = acc_sc[g].T.astype(BF16)

    def older_keys_matter():
        return jnp.max(c_sc[...]) > NEGLIGIBLE_LOG2

    _for_each_query_block(n_q, group, [scores, log_sigmoids, suffix_sums, weights, accumulate],
                          begin, finish, older_keys_matter)


def _attention(qT, k, vT, *, lam4=None, subln_g=None, lambda_init=None):
    b, n_pairs, nblk, _, blk = qT.shape
    s = nblk * blk
    d = k.shape[-1]
    diff = lam4 is not None
    group = n_pairs
    mode = dict(pipeline_mode=pl.Buffered(1))
    qv_spec = pl.BlockSpec((1, group, nblk, PAIR, blk), lambda bi, h: (bi, h, 0, 0, 0), **mode)
    k_spec = pl.BlockSpec((1, s, group * PAIR), lambda bi, h: (bi, 0, h), **({} if diff else mode))
    o_spec = pl.BlockSpec((1, s, group * PAIR), lambda bi, h: (bi, 0, h), **mode)
    common = dict(
        grid=(b, n_pairs // group),
        out_specs=o_spec,
        out_shape=jax.ShapeDtypeStruct((b, s, d), BF16),
        compiler_params=pltpu.CompilerParams(
            dimension_semantics=("arbitrary", "arbitrary"), vmem_limit_bytes=VMEM_LIMIT),
    )
    qpad =pltpu.VMEM((group, PAIR, 2 * blk), BF16)
    stat = pltpu.VMEM((group, 1, 2 * blk), F32)
    tile_f32 = pltpu.VMEM((group, blk, 2 * blk), F32)
    tile_bf16 = pltpu.VMEM((group, blk, 2 * blk), BF16)
    if lam4 is not None:
        return pl.pallas_call(
            functools.partial(_diff_attn_kernel, lambda_init=lambda_init),
            in_specs=[pl.BlockSpec(lam4.shape, lambda bi, h: (0, 0)),
                      pl.BlockSpec((PAIR, 1), lambda bi, h: (0, 0)),
                      qv_spec, k_spec, qv_spec],
            scratch_shapes=[qpad, tile_f32, tile_bf16, stat, stat,
                            pltpu.VMEM((group, PAIR + BF16_ROWS, 2 * blk), F32)],
            name="diff_attention", **common,
        )(lam4, subln_g.reshape(PAIR, 1), qT, k, vT)
    return pl.pallas_call(
        _stick_attn_kernel,
        in_specs=[qv_spec, k_spec, qv_spec],
        scratch_shapes=[qpad, pltpu.VMEM((blk, blk), BF16), tile_f32, tile_f32, tile_bf16, stat,
                        tile_f32, tile_bf16, stat, pltpu.VMEM((group, PAIR, blk), F32)],
        name="stick_attention", **common,
    )(qT, k, vT)


def _post_kernel(o_ref, x_ref, wo_ref, gpost_ref, gpre_ref, wgu_ref, wd_ref, gfpost_ref, out_ref,
                 *, ff_bounds):
    d_ff = wd_ref.shape[0]
    tm = x_ref.shape[0]
    halves = [slice(r, r + POST_CHAIN_ROWS) for r in range(0, tm, POST_CHAIN_ROWS)]

    def mix(rows, _):
        m = jnp.dot(o_ref[rows], wo_ref[...], preferred_element_type=F32)
        x1 = x_ref[rows] + _rms(m) * gpost_ref[...]
        return x1, (_rms(x1) * gpre_ref[...]).astype(BF16), None

    def ffn_chunk(lo, hi):
        def stage(rows, vals):
            x1, hf, f = vals
            gate = jnp.dot(hf, wgu_ref[:, lo:hi], preferred_element_type=F32)
            up = jnp.dot(hf, wgu_ref[:, d_ff + lo:d_ff + hi], preferred_element_type=F32)
            h = (gate * jax.nn.sigmoid(gate) * up).astype(BF16)
            part = jnp.dot(h, wd_ref[lo:hi, :], preferred_element_type=F32)
            return x1, hf, part if f is None else f + part
        return stage

    def store(rows, vals):
        x1, _, f = vals
        out_ref[rows] = x1 + _rms(f) * gfpost_ref[...]

    stages = [mix] + [ffn_chunk(lo, hi) for lo, hi in zip(ff_bounds, ff_bounds[1:])] + [store]
    vals = [None] * len(halves)
    for step in range(len(halves) + len(stages) - 1):
        for k, stage in enumerate(stages):
            i = step - k
            if 0 <= i < len(halves):
                vals[i] = stage(halves[i], vals[i])


def _post(o, x, wo, gpost, gpre, wgu, wd, gfpost):
    n, d = x.shape
    d_ff = wd.shape[0]
    tm = POST_ROWS
    ff_bounds = (0, -(-d_ff // (2 * MXU_TILE)) * MXU_TILE, d_ff)
    rows = lambda i: (i, 0)
    const = lambda i: (0, 0)
    resident = functools.partial(pl.BlockSpec, index_map=const, pipeline_mode=pl.Buffered(1))
    gain = pl.BlockSpec((1, d), const)
    return pl.pallas_call(
        functools.partial(_post_kernel, ff_bounds=ff_bounds),
        grid=(n // tm,),
        in_specs=[pl.BlockSpec((tm, d), rows), pl.BlockSpec((tm, d), rows),
                  resident((d, d)), gain, gain,
                  resident((d, 2 * d_ff)), resident((d_ff, d)), gain],
        out_specs=pl.BlockSpec((tm, d), rows),
        out_shape=jax.ShapeDtypeStruct((n, d), F32),
        compiler_params=pltpu.CompilerParams(
            dimension_semantics=("arbitrary",), vmem_limit_bytes=VMEM_LIMIT),
        name="post_swiglu",
    )(o, x, wo, gpost.reshape(1, d), gpre.reshape(1, d), wgu, wd, gfpost.reshape(1, d))


def _rope_tables(seq):
    pos = jnp.arange(seq, dtype=F32)
    inv_freq = ROPE_THETA ** (-jnp.arange(0, 2 * ROT_HALF, 2, dtype=F32) / (2 * ROT_HALF))
    ang = pos[:, None] * inv_freq[None, :]
    cos, sin = jnp.cos(ang), jnp.sin(ang)
    ones = jnp.ones((seq, HEAD_DIM - 2 * ROT_HALF), F32)
    zeros = jnp.zeros((seq, HEAD_DIM - 2 * ROT_HALF), F32)
    z8 = jnp.zeros_like(sin)
    ck = jnp.tile(jnp.concatenate([cos, cos, ones], axis=1), (1, 2))
    sk1 = jnp.tile(jnp.concatenate([-sin, z8, zeros], axis=1), (1, 2))
    sk2 = jnp.tile(jnp.concatenate([z8, sin, zeros], axis=1), (1, 2))
    return cos.T, sin.T, ck, sk1, sk2


def kernel(x, a_w_qkv, a_w_o, a_lambda_q1, a_lambda_k1, a_lambda_q2, a_lambda_k2, a_subln_g,
           kv_norm_g, kv_w, b_w_q, b_w_o, mix_pre_g, mix_post_g, ffn_pre_g, ffn_post_g,
           ffn_w_gate_up, ffn_w_down):
    b, s, d = x.shape
    assert a_w_qkv.shape[0] == 1 and b_w_q.shape[0] == 1 and mix_pre_g.shape[0] == 2
    assert s % PROJ_ROWS == 0 and (b * s) % POST_ROWS == 0 and d % (PAIR * ATT_GROUP) == 0

    w = a_w_qkv[0]
    q, k, vT = _project(x, mix_pre_g[0], mix_pre_g[0],
                        w[:, :d].T.astype(BF16), w[:, d:2 * d].astype(BF16),
                        w[:, 2 * d:].T.astype(BF16), rope_tables=_rope_tables(s))
    lam4 = jnp.stack([a_lambda_q1[0], a_lambda_k1[0], a_lambda_q2[0], a_lambda_k2[0]])
    lambda_init = LAMBDA_INIT_BASE - LAMBDA_INIT_AMP * math.exp(-LAMBDA_INIT_RATE * 0)
    o = _attention(q, k, vT, lam4=lam4, subln_g=a_subln_g[0], lambda_init=lambda_init)
    x = _post(o.reshape(b * s, d), x.reshape(b * s, d), a_w_o[0].astype(BF16), mix_post_g[0],
              ffn_pre_g[0], ffn_w_gate_up[0].astype(BF16), ffn_w_down[0].astype(BF16),
              ffn_post_g[0]).reshape(b, s, d)

    q, k, vT = _project(x, mix_pre_g[1], kv_norm_g,
                        b_w_q[0].T.astype(BF16), kv_w[:, :d].astype(BF16),
                        kv_w[:, d:].T.astype(BF16))
    o = _attention(q, k, vT)
    x = _post(o.reshape(b * s, d), x.reshape(b * s, d), b_w_o[0].astype(BF16), mix_post_g[1],
              ffn_pre_g[1], ffn_w_gate_up[1].astype(BF16), ffn_w_down[1].astype(BF16),
              ffn_post_g[1]).reshape(b, s, d)
    return x
```
